```python
import math
import jax
import jax.numpy as jnp
from jax import lax

D_MODEL = 4096
BATCH = 2
SEQ = 4096
DEPTH = 2

GRID_W = 64
CTX_LEN = 256
HEAD_DIM = 128
RET_HEADS = 8
RET_CHUNK = 128
GDN_HEADS = 8
GDN_CHUNK = 64
CONV_W = 5
MLA_HEADS = 16
Q_LORA = 1024
KV_LORA = 512
NOPE_DIM = 128
ROPE_DIM = 64
V_DIM = 128
MLA_QK = NOPE_DIM + ROPE_DIM
ATTN_BLOCK = 128
PEER_HEADS = 8
N_KEYS = 128
N_EXPERTS = N_KEYS * N_KEYS
PEER_TOPK = 16
PEER_QDIM = 256
PEER_BLOCK = 64
ROPE_BASE = 10000.0
NORM_EPS = 1e-6
RET_W = RET_HEADS * HEAD_DIM
GDN_W = GDN_HEADS * HEAD_DIM
MLA_W = MLA_HEADS * V_DIM
MIX_W = RET_W + GDN_W + MLA_W
IN_SIZES = (RET_W,) * 4 + (GDN_W,) * 4 + (GDN_HEADS,) * 4 + (Q_LORA, KV_LORA, ROPE_DIM)
IN_W = 4 * RET_W + 4 * GDN_W + 4 * GDN_HEADS + Q_LORA + KV_LORA + ROPE_DIM

kernel_name = 'hybrid_ret_gdn_mla_peer_dit_trunk'


def rms_norm(x, w):
    xf = x.astype(jnp.float32)
    y = xf * lax.rsqrt(jnp.mean(xf * xf, axis=-1, keepdims=True) + NORM_EPS)
    return (y * w.astype(jnp.float32)).astype(x.dtype)


def l2_norm(t):
    return t * lax.rsqrt(jnp.sum(t * t, axis=-1, keepdims=True) + NORM_EPS)


def head_group_norm(o):
    mu = jnp.mean(o, axis=-1, keepdims=True)
    var = jnp.mean(jnp.square(o - mu), axis=-1, keepdims=True)
    return (o - mu) * lax.rsqrt(var + NORM_EPS)


def modulate(h, shift, scale):
    return h * (1 + scale) + shift


def split_cols(t, sizes):
    parts, start = [], 0
    for s in sizes:
        parts.append(t[..., start:start + s])
        start += s
    return parts


def to_heads(t, n_heads):
    b, n, _ = t.shape
    return t.reshape(b, n, n_heads, -1).transpose(0, 2, 1, 3)


def from_heads(t):
    b, h, n, d = t.shape
    return t.transpose(0, 2, 1, 3).reshape(b, n, h * d)


def axial_rope_tables(row, col, rot_dim):
    n_freq = rot_dim // 4
    inv_freq = ROPE_BASE ** (-jnp.arange(n_freq, dtype=jnp.float32) / n_freq)
    ang = jnp.concatenate([row[:, None] * inv_freq, col[:, None] * inv_freq], axis=-1)
    return jnp.cos(ang), jnp.sin(ang)


def apply_rope(t, cos, sin):
    tf = t.astype(jnp.float32)
    half = tf.shape[-1] // 2
    t1, t2 = tf[..., :half], tf[..., half:]
    return jnp.concatenate([t1 * cos - t2 * sin, t1 * sin + t2 * cos], axis=-1).astype(t.dtype)


def short_conv(u, w):
    pad = CONV_W // 2
    return lax.conv_general_dilated(u, w[:, None, :].astype(u.dtype), window_strides=(1,),
                                    padding=((pad, pad),), dimension_numbers=('NWC', 'WIO', 'NWC'),
                                    feature_group_count=u.shape[-1])


def bidirectional(scan_f, scan_b, args_f, args_b, n_ctx, s0):
    n_tot = args_f[0].shape[2]

    def seg(args, lo, hi, rev):
        parts = [a[:, :, lo:hi] for a in args]
        return [jnp.flip(a, axis=2) for a in parts] if rev else parts

    o_cf, s_f = scan_f(*seg(args_f, 0, n_ctx, False), s0)
    o_cb, s_b = scan_b(*seg(args_b, 0, n_ctx, True), s0)
    o_xf, _ = scan_f(*seg(args_f, n_ctx, n_tot, False), s_f)
    o_xb, _ = scan_b(*seg(args_b, n_ctx, n_tot, True), s_b)
    o_c = o_cf + jnp.flip(o_cb, axis=2)
    o_x = o_xf + jnp.flip(o_xb, axis=2)
    return jnp.concatenate([o_c, o_x], axis=2)


def retention_scan(q, k, v, log_gamma, s0):
    b, h, n, d = q.shape
    nc = n // RET_CHUNK
    pos = jnp.arange(RET_CHUNK, dtype=jnp.float32)
    diff = pos[:, None] - pos[None, :]
    intra = jnp.where(diff >= 0, jnp.exp(log_gamma[:, None, None] * jnp.maximum(diff, 0.0)), 0.0)
    q_dec = jnp.exp(log_gamma[:, None] * (pos + 1.0))[..., None]
    k_dec = jnp.exp(log_gamma[:, None] * (RET_CHUNK - 1.0 - pos))[..., None]
    chunk_dec = jnp.exp(log_gamma * RET_CHUNK)[:, None, None]

    def chunks(t):
        return jnp.moveaxis(t.reshape(b, h, nc, RET_CHUNK, d), 2, 0)

    def step(state, inp):
        qi, ki, vi = inp
        scores = jnp.einsum('bhid,bhjd->bhij', qi, ki) * intra
        out = jnp.einsum('bhij,bhjv->bhiv', scores, vi) + jnp.einsum('bhid,bhdv->bhiv', qi * q_dec, state)
        state = state * chunk_dec + jnp.einsum('bhjd,bhjv->bhdv', ki * k_dec, vi)
        return state, out

    s_fin, out = lax.scan(step, s0, (chunks(q), chunks(k), chunks(v)))
    return jnp.moveaxis(out, 0, 2).reshape(b, h, n, d), s_fin


def gdn_scan(q, k, v, g, beta, s0):
    b, h, n, d = q.shape
    nc = n // GDN_CHUNK
    cl = GDN_CHUNK
    q, k, v = (t.reshape(b, h, nc, cl, d) for t in (q, k, v))
    g = jnp.cumsum(g.reshape(b, h, nc, cl), axis=-1)
    beta = beta.reshape(b, h, nc, cl)
    idx = jnp.arange(cl)
    causal = idx[:, None] >= idx[None, :]
    strict = idx[:, None] > idx[None, :]
    diff = g[..., :, None] - g[..., None, :]
    decay = jnp.where(causal, jnp.exp(jnp.where(causal, diff, 0.0)), 0.0)
    k_beta = k * beta[..., None]
    a_mat = jnp.where(strict, jnp.einsum('bhnid,bhnjd->bhnij', k_beta, k) * decay, 0.0)
    eye = jnp.eye(cl, dtype=q.dtype)
    t_mat = lax.linalg.triangular_solve(eye + a_mat, jnp.broadcast_to(eye, a_mat.shape),
                                        left_side=True, lower=True, unit_diagonal=True)
    u = jnp.einsum('bhnij,bhnjd->bhnid', t_mat, v * beta[..., None])
    w = jnp.einsum('bhnij,bhnjd->bhnid', t_mat, k_beta * jnp.exp(g)[..., None])
    attn = jnp.einsum('bhnid,bhnjd->bhnij', q, k) * decay
    q_dec = q * jnp.exp(g)[..., None]
    k_dec = k * jnp.exp(g[..., -1:] - g)[..., None]
    chunk_dec = jnp.exp(g[..., -1])

    def step(state, inp):
        u_i, w_i, qd_i, kd_i, at_i, cd_i = inp
        v_new = u_i - jnp.einsum('bhck,bhkv->bhcv', w_i, state)
        out = jnp.einsum('bhck,bhkv->bhcv', qd_i, state) + jnp.einsum('bhij,bhjv->bhiv', at_i, v_new)
        state = state * cd_i[..., None, None] + jnp.einsum('bhck,bhcv->bhkv', kd_i, v_new)
        return state, out

    xs = tuple(jnp.moveaxis(t, 2, 0) for t in (u, w, q_dec, k_dec, attn, chunk_dec))
    s_fin, out = lax.scan(step, s0, xs)
    return jnp.moveaxis(out, 0, 2).reshape(b, h, n, d), s_fin


def retention_mixer(rq, rk, rv, rg, norm_w, n_ctx, cos, sin):
    f32 = jnp.float32
    q = to_heads(rq.astype(f32), RET_HEADS)
    k = to_heads(rk.astype(f32), RET_HEADS) * HEAD_DIM ** -0.5
    v = to_heads(rv.astype(f32), RET_HEADS)
    q = jnp.concatenate([q[:, :, :n_ctx], apply_rope(q[:, :, n_ctx:], cos, sin)], axis=2)
    k = jnp.concatenate([k[:, :, :n_ctx], apply_rope(k[:, :, n_ctx:], cos, sin)], axis=2)
    log_g_f = jnp.log1p(-jnp.exp2(-5.0 - jnp.arange(RET_HEADS, dtype=f32)))
    log_g_b = jnp.flip(log_g_f)
    s0 = jnp.zeros(q.shape[:2] + (HEAD_DIM, HEAD_DIM), f32)
    o = bidirectional(lambda a, b_, c_, s: retention_scan(a, b_, c_, log_g_f, s),
                      lambda a, b_, c_, s: retention_scan(a, b_, c_, log_g_b, s),
                      (q, k, v), (q, k, v), n_ctx, s0)
    o = from_heads(head_group_norm(o))
    return (o * norm_w.astype(f32) * jax.nn.silu(rg.astype(f32))).astype(rq.dtype)


def gdn_mixer(gq, gk, gv, gz, a_f, a_b, b_f, b_b, conv_w, a_log, dt_bias, norm_w, n_ctx):
    f32 = jnp.float32
    qkv = jnp.concatenate([gq, gk, gv], axis=-1)
    qkv = jnp.concatenate([short_conv(qkv[:, :n_ctx], conv_w), short_conv(qkv[:, n_ctx:], conv_w)], axis=1)
    qkv = jax.nn.silu(qkv.astype(f32))
    q, k, v = (to_heads(t, GDN_HEADS) for t in jnp.split(qkv, 3, axis=-1))
    q = l2_norm(q) * HEAD_DIM ** -0.5
    k = l2_norm(k)

    def gates(a, bb, direction):
        g = -jnp.exp(a_log[direction].astype(f32)) * jax.nn.softplus(a.astype(f32) + dt_bias[direction].astype(f32))
        beta = jax.nn.sigmoid(bb.astype(f32))
        return g.transpose(0, 2, 1), beta.transpose(0, 2, 1)

    g_f, be_f = gates(a_f, b_f, 0)
    g_b, be_b = gates(a_b, b_b, 1)
    s0 = jnp.zeros(q.shape[:2] + (HEAD_DIM, HEAD_DIM), f32)
    o = bidirectional(gdn_scan, gdn_scan, (q, k, v, g_f, be_f), (q, k, v, g_b, be_b), n_ctx, s0)
    o = from_heads(rms_norm(o, norm_w))
    return (o * jax.nn.silu(gz.astype(f32))).astype(gz.dtype)


def softmax_attend(q, k, v):
    s = jnp.einsum('bhqd,bhkd->bhqk', q, k).astype(jnp.float32) * MLA_QK ** -0.5
    p = jax.nn.softmax(s, axis=-1).astype(v.dtype)
    return jnp.einsum('bhqk,bhkd->bhqd', p, v)


def mla_mixer(c_q, c_kv, k_rope, cq_norm_w, ckv_norm_w, w_uq, w_ukv, q_norm_w, k_norm_w,
              cos, sin, n_ctx, with_ctx):
    b, n, _ = c_q.shape
    q = (rms_norm(c_q, cq_norm_w) @ w_uq).reshape(b, n, MLA_HEADS, MLA_QK)
    kv = (rms_norm(c_kv, ckv_norm_w) @ w_ukv).reshape(b, n, MLA_HEADS, NOPE_DIM + V_DIM)
    k = jnp.concatenate([kv[..., :NOPE_DIM],
                         jnp.broadcast_to(k_rope[:, :, None, :], (b, n, MLA_HEADS, ROPE_DIM))], axis=-1)
    q = rms_norm(q, q_norm_w).transpose(0, 2, 1, 3)
    k = rms_norm(k, k_norm_w).transpose(0, 2, 1, 3)
    v = kv[..., NOPE_DIM:].transpose(0, 2, 1, 3)

    def rope_tail(t):
        return jnp.concatenate([t[..., :NOPE_DIM], apply_rope(t[..., NOPE_DIM:], cos, sin)], axis=-1)

    k_c, v_c = k[:, :, :n_ctx], v[:, :, :n_ctx]
    q_x = rope_tail(q[:, :, n_ctx:])
    k_all = jnp.concatenate([rope_tail(k[:, :, n_ctx:]), k_c], axis=2)
    v_all = jnp.concatenate([v[:, :, n_ctx:], v_c], axis=2)
    n_lat = n - n_ctx
    n_blk = n_lat // ATTN_BLOCK
    q_blocks = jnp.moveaxis(q_x.reshape(b, MLA_HEADS, n_blk, ATTN_BLOCK, MLA_QK), 2, 0)
    o_x = lax.map(lambda qb: softmax_attend(qb, k_all, v_all), q_blocks)
    o_x = from_heads(jnp.moveaxis(o_x, 0, 2).reshape(b, MLA_HEADS, n_lat, V_DIM))
    o_c = from_heads(softmax_attend(q[:, :, :n_ctx], k_c, v_c)) if with_ctx else None
    return o_c, o_x


def peer_ffn(h, wq, sub_keys, u_tab, v_tab):
    b, n, d = h.shape
    blocks = h.reshape(-1, PEER_BLOCK, d)

    def block(xb):
        nt = xb.shape[0]
        q = (xb @ wq).reshape(nt, PEER_HEADS, 2, PEER_QDIM // 2)
        s = jnp.einsum('thpd,hpkd->thpk', q, sub_keys.astype(q.dtype)).astype(jnp.float32)
        s_top, i_top = lax.top_k(s, PEER_TOPK)
        n_cand = PEER_TOPK * PEER_TOPK
        cand_s = (s_top[:, :, 0, :, None] + s_top[:, :, 1, None, :]).reshape(nt, PEER_HEADS, n_cand)
        cand_i = (i_top[:, :, 0, :, None] * N_KEYS + i_top[:, :, 1, None, :]).reshape(nt, PEER_HEADS, n_cand)
        best_s, best_pos = lax.top_k(cand_s, PEER_TOPK)
        expert = jnp.take_along_axis(cand_i, best_pos, axis=-1)
        gate = jax.nn.softmax(best_s, axis=-1)
        act = jax.nn.gelu(jnp.einsum('thkd,td->thk', u_tab[expert], xb).astype(jnp.float32), approximate=False)
        coef = (gate * act).astype(xb.dtype)
        return jnp.einsum('thk,thkd->td', coef, v_tab[expert])

    return lax.map(block, blocks).reshape(b, n, d)


def trunk_layer(x, ctx, mod_x, mod_c, lp, rope_ret, rope_mla, update_ctx):
    n_ctx = ctx.shape[1]
    sh1, sc1, g1, sh2, sc2, g2 = (m[:, None, :] for m in jnp.split(mod_x, 6, axis=-1))
    csh1, csc1, cg1, csh2, csc2, cg2 = jnp.split(mod_c, 6, axis=-1)
    h = jnp.concatenate([modulate(rms_norm(ctx, lp['norm1_w']), csh1, csc1),
                         modulate(rms_norm(x, lp['norm1_w']), sh1, sc1)], axis=1)
    (rq, rk, rv, rg, gq, gk, gv, gz, a_f, a_b, b_f, b_b,
     c_q, c_kv, k_rope) = split_cols(h @ lp['w_in'], IN_SIZES)
    ret_o = retention_mixer(rq, rk, rv, rg, lp['ret_norm_w'], n_ctx, *rope_ret)
    gdn_o = gdn_mixer(gq, gk, gv, gz, a_f, a_b, b_f, b_b, lp['conv_w'], lp['gdn_a_log'],
                      lp['gdn_dt_bias'], lp['gdn_norm_w'], n_ctx)
    mla_c, mla_x = mla_mixer(c_q, c_kv, k_rope, lp['cq_norm_w'], lp['ckv_norm_w'], lp['w_uq'], lp['w_ukv'],
                             lp['q_norm_w'], lp['k_norm_w'], *rope_mla, n_ctx, update_ctx)
    if update_ctx:
        mixed = jnp.concatenate([ret_o, gdn_o, jnp.concatenate([mla_c, mla_x], axis=1)], axis=-1)
        y = mixed @ lp['w_out']
        ctx = ctx + cg1 * y[:, :n_ctx]
        x = x + g1 * y[:, n_ctx:]
        f = jnp.concatenate([modulate(rms_norm(ctx, lp['norm2_w']), csh2, csc2),
                             modulate(rms_norm(x, lp['norm2_w']), sh2, sc2)], axis=1)
        f = peer_ffn(f, lp['peer_wq'], lp['peer_keys'], lp['peer_u'], lp['peer_v'])
        ctx = ctx + cg2 * f[:, :n_ctx]
        x = x + g2 * f[:, n_ctx:]
    else:
        mixed = jnp.concatenate([ret_o[:, n_ctx:], gdn_o[:, n_ctx:], mla_x], axis=-1)
        x = x + g1 * (mixed @ lp['w_out'])
        f = peer_ffn(modulate(rms_norm(x, lp['norm2_w']), sh2, sc2),
                     lp['peer_wq'], lp['peer_keys'], lp['peer_u'], lp['peer_v'])
        x = x + g2 * f
    return x, ctx


def setup_inputs(seed: int = 0) -> dict:
    key = jax.random.key(seed)
    k = jax.random.split(key, 25)
    f32 = jnp.float32
    d = D_MODEL

    def nrm(i, shape, std):
        return std * jax.random.normal(k[i], shape, f32)

    def gain(i, shape):
        return 1.0 + 0.02 * jax.random.normal(k[i], shape, f32)

    dt = jnp.exp(jax.random.uniform(k[11], (DEPTH, 2, GDN_HEADS), f32, math.log(1e-3), math.log(1e-1)))
    return {
        'x': nrm(0, (BATCH, SEQ, d), 1.0),
        'c': nrm(1, (BATCH, d), 1.0),
        'ctx': nrm(2, (BATCH, CTX_LEN, d), 1.0),
        'c_ctx': nrm(3, (d,), 1.0),
        'ada_w': nrm(4, (DEPTH, d, 6 * d), 0.3 * d ** -0.5),
        'ada_b': nrm(5, (DEPTH, 6 * d), 0.02),
        'norm1_w': gain(6, (DEPTH, d)),
        'norm2_w': gain(7, (DEPTH, d)),
        'w_in': nrm(8, (DEPTH, d, IN_W), d ** -0.5),
        'conv_w': nrm(9, (DEPTH, CONV_W, 3 * GDN_W), CONV_W ** -0.5),
        'gdn_a_log': jnp.log(jax.random.uniform(k[10], (DEPTH, 2, GDN_HEADS), f32, 1.0, 16.0)),
        'gdn_dt_bias': dt + jnp.log(-jnp.expm1(-dt)),
        'gdn_norm_w': gain(12, (DEPTH, HEAD_DIM)),
        'ret_norm_w': gain(13, (DEPTH, RET_W)),
        'cq_norm_w': gain(14, (DEPTH, Q_LORA)),
        'ckv_norm_w': gain(15, (DEPTH, KV_LORA)),
        'w_uq': nrm(16, (DEPTH, Q_LORA, MLA_HEADS * MLA_QK), Q_LORA ** -0.5),
        'w_ukv': nrm(17, (DEPTH, KV_LORA, MLA_HEADS * (NOPE_DIM + V_DIM)), KV_LORA ** -0.5),
        'q_norm_w': gain(18, (DEPTH, MLA_QK)),
        'k_norm_w': gain(19, (DEPTH, MLA_QK)),
        'w_out': nrm(20, (DEPTH, MIX_W, d), MIX_W ** -0.5),
        'peer_wq': nrm(21, (DEPTH, d, PEER_HEADS * PEER_QDIM), d ** -0.5),
        'peer_keys': nrm(22, (DEPTH, PEER_HEADS, 2, N_KEYS, PEER_QDIM // 2), (PEER_QDIM // 2) ** -0.5),
        'peer_u': nrm(23, (DEPTH, N_EXPERTS, d), d ** -0.5),
        'peer_v': nrm(24, (DEPTH, N_EXPERTS, d), PEER_HEADS ** -0.5),
    }


def reference(x, c, ctx, c_ctx, ada_w, ada_b, norm1_w, norm2_w, w_in, conv_w, gdn_a_log, gdn_dt_bias,
              gdn_norm_w, ret_norm_w, cq_norm_w, ckv_norm_w, w_uq, w_ukv, q_norm_w, k_norm_w, w_out,
              peer_wq, peer_keys, peer_u, peer_v):
    n_lat = x.shape[1]
    n_rows = n_lat // GRID_W
    row = jnp.repeat(jnp.arange(n_rows, dtype=jnp.float32), GRID_W, total_repeat_length=n_lat)
    col = (jnp.arange(n_lat) % GRID_W).astype(jnp.float32)
    rope_ret = axial_rope_tables(row, col, HEAD_DIM)
    rope_mla = axial_rope_tables(row, col, ROPE_DIM)
    silu_c = jax.nn.silu(c)
    silu_cc = jax.nn.silu(c_ctx)
    for i in range(DEPTH):
        mod_x = silu_c @ ada_w[i] + ada_b[i]
        mod_c = silu_cc @ ada_w[i] + ada_b[i]
        lp = {
            'norm1_w': norm1_w[i], 'norm2_w': norm2_w[i], 'w_in': w_in[i], 'conv_w': conv_w[i],
            'gdn_a_log': gdn_a_log[i], 'gdn_dt_bias': gdn_dt_bias[i], 'gdn_norm_w': gdn_norm_w[i],
            'ret_norm_w': ret_norm_w[i], 'cq_norm_w': cq_norm_w[i], 'ckv_norm_w': ckv_norm_w[i],
            'w_uq': w_uq[i], 'w_ukv': w_ukv[i], 'q_norm_w': q_norm_w[i], 'k_norm_w': k_norm_w[i],
            'w_out': w_out[i], 'peer_wq': peer_wq[i], 'peer_keys': peer_keys[i],
            'peer_u': peer_u[i], 'peer_v': peer_v[i],
        }
        x, ctx = trunk_layer(x, ctx, mod_x, mod_c, lp, rope_ret, rope_mla, i < DEPTH - 1)
    return x
```

```python
import functools
import math

import jax
import jax.numpy as jnp
from jax import lax
from jax.experimental import pallas as pl
from jax.experimental.pallas import tpu as pltpu

D_MODEL = 4096
DEPTH = 2
GRID_W = 64
HEAD_DIM = 128
RET_HEADS = 8
RET_CHUNK = 128
GDN_HEADS = 8
GDN_CHUNK = 64
CONV_W = 5
MLA_HEADS = 16
Q_LORA = 1024
KV_LORA = 512
NOPE_DIM = 128
ROPE_DIM = 64
V_DIM = 128
MLA_QK = NOPE_DIM + ROPE_DIM
ATTN_BLOCK = 128
PEER_HEADS = 8
N_KEYS = 128
N_EXPERTS = N_KEYS * N_KEYS
PEER_TOPK = 16
PEER_QDIM = 256
ROPE_BASE = 10000.0
NORM_EPS = 1e-6
RET_W = RET_HEADS * HEAD_DIM
GDN_W = GDN_HEADS * HEAD_DIM
MLA_W = MLA_HEADS * V_DIM
MIX_W = RET_W + GDN_W + MLA_W
IN_SIZES = (RET_W,) * 4 + (GDN_W,) * 4 + (GDN_HEADS,) * 4 + (Q_LORA, KV_LORA, ROPE_DIM)

LANES = 128
VMEM_LIMIT = 56 * 1024 * 1024

F32 = jnp.float32
BF16 = jnp.bfloat16


def _mm_kernel(a_ref, b_ref, o_ref):
    o_ref[...] = jnp.dot(a_ref[...].astype(BF16), b_ref[...].astype(BF16),
                         preferred_element_type=F32).astype(o_ref.dtype)


def _pick_tile(n, target, align):
    best = n
    for t in range(align, min(n, target) + 1, align):
        if n % t == 0:
            best = t
    return best


def matmul(a, b, out_dtype=F32, tm=512, tn=512):
    m, k = a.shape
    n = b.shape[1]
    tm = _pick_tile(m, tm, 8)
    tn = _pick_tile(n, tn, LANES)
    return pl.pallas_call(
        _mm_kernel,
        grid=(m // tm, n // tn),
        in_specs=[pl.BlockSpec((tm, k), lambda i, j: (i, 0)),
                  pl.BlockSpec((k, tn), lambda i, j: (0, j))],
        out_specs=pl.BlockSpec((tm, tn), lambda i, j: (i, j)),
        out_shape=jax.ShapeDtypeStruct((m, n), out_dtype),
        compiler_params=pltpu.CompilerParams(
            dimension_semantics=("parallel", "parallel"), vmem_limit_bytes=VMEM_LIMIT),
    )(a, b)


PEER_TT = 512
PEER_EB = 512


def _gelu(a):
    return 0.5 * a * (1.0 + lax.erf(a * (2.0 ** -0.5)))


def _peer_kernel(x_ref, u_ref, vt_ref, s1_ref, e1_ref, s2_ref, e2_ref, tau_ref, o_ref, acc_ref):
    e = pl.program_id(1)

    @pl.when(e == 0)
    def _():
        acc_ref[...] = jnp.zeros_like(acc_ref)

    act_t = lax.dot_general(u_ref[...], x_ref[...], (((1,), (1,)), ((), ())),
                            preferred_element_type=F32)
    rows = []
    for jj in range(PEER_EB // N_KEYS):
        cols = []
        for ts in range(PEER_TT // LANES):
            lane = slice(ts * LANES, (ts + 1) * LANES)
            g = jnp.zeros((N_KEYS, LANES), F32)
            for h in range(PEER_HEADS):
                s1row = s1_ref[jj, h:h + 1, lane]
                e1row = e1_ref[jj, h:h + 1, lane]
                t = s2_ref[h, :, lane] + s1row
                g = g + jnp.where(t >= tau_ref[h:h + 1, lane], e2_ref[h, :, lane] * e1row, 0.0)
            a = act_t[jj * N_KEYS:(jj + 1) * N_KEYS, lane]
            cols.append((_gelu(a) * g).astype(BF16))
        rows.append(jnp.concatenate(cols, axis=1))
    c_t = jnp.concatenate(rows, axis=0)
    acc_ref[...] += jnp.dot(vt_ref[...], c_t, preferred_element_type=F32)

    @pl.when(e == pl.num_programs(1) - 1)
    def _():
        o_ref[...] = acc_ref[...].T.astype(o_ref.dtype)


def peer_dense(x, u, vt, s1r, e1r, s2t, e2t, tau):
    t, d = x.shape
    n_e = u.shape[0]
    jb = PEER_EB // N_KEYS
    return pl.pallas_call(
        _peer_kernel,
        grid=(t // PEER_TT, n_e // PEER_EB),
        in_specs=[
            pl.BlockSpec((PEER_TT, d), lambda i, e: (i, 0)),
            pl.BlockSpec((PEER_EB, d), lambda i, e: (e, 0)),
            pl.BlockSpec((d, PEER_EB), lambda i, e: (0, e)),
            pl.BlockSpec((jb, PEER_HEADS, PEER_TT), lambda i, e: (e, 0, i)),
            pl.BlockSpec((jb, PEER_HEADS, PEER_TT), lambda i, e: (e, 0, i)),
            pl.BlockSpec((PEER_HEADS, N_KEYS, PEER_TT), lambda i, e: (0, 0, i)),
            pl.BlockSpec((PEER_HEADS, N_KEYS, PEER_TT), lambda i, e: (0, 0, i)),
            pl.BlockSpec((PEER_HEADS, PEER_TT), lambda i, e: (0, i)),
        ],
        out_specs=pl.BlockSpec((PEER_TT, d), lambda i, e: (i, 0)),
        out_shape=jax.ShapeDtypeStruct((t, d), BF16),
        scratch_shapes=[pltpu.VMEM((d, PEER_TT), F32)],
        compiler_params=pltpu.CompilerParams(
            dimension_semantics=("parallel", "arbitrary"), vmem_limit_bytes=VMEM_LIMIT),
    )(x, u, vt, s1r, e1r, s2t, e2t, tau)


def peer_ffn(f, wq, sub_keys, u_bf, vt_bf):
    t = f.shape[0]
    f_bf = f.astype(BF16)
    q = matmul(f_bf, wq, F32).reshape(t, PEER_HEADS, 2, PEER_QDIM // 2)
    s = jnp.einsum('thpd,hpkd->thpk', q, sub_keys).astype(F32)
    s_top, _ = lax.top_k(s, PEER_TOPK)
    cand = (s_top[:, :, 0, :, None] + s_top[:, :, 1, None, :]).reshape(t, PEER_HEADS, PEER_TOPK * PEER_TOPK)
    best, _ = lax.top_k(cand, PEER_TOPK)
    tau = best[..., -1]
    m1 = s_top[:, :, 0, 0]
    m2 = s_top[:, :, 1, 0]
    z = jnp.sum(jnp.exp(best - (m1 + m2)[..., None]), axis=-1)
    e1 = jnp.exp(s[:, :, 0, :] - m1[..., None])
    e2 = jnp.exp(s[:, :, 1, :] - m2[..., None]) / z[..., None]
    s1r = s[:, :, 0, :].transpose(2, 1, 0)
    e1r = e1.transpose(2, 1, 0)
    s2t = s[:, :, 1, :].transpose(1, 2, 0)
    e2t = e2.transpose(1, 2, 0)
    return peer_dense(f_bf, u_bf, vt_bf, s1r, e1r, s2t, e2t, tau.T)


def rms_norm(x, w):
    xf = x.astype(F32)
    y = xf * lax.rsqrt(jnp.mean(xf * xf, axis=-1, keepdims=True) + NORM_EPS)
    return (y * w.astype(F32)).astype(x.dtype)


def l2_norm(t):
    return t * lax.rsqrt(jnp.sum(t * t, axis=-1, keepdims=True) + NORM_EPS)


def head_group_norm(o):
    mu = jnp.mean(o, axis=-1, keepdims=True)
    var = jnp.mean(jnp.square(o - mu), axis=-1, keepdims=True)
    return (o - mu) * lax.rsqrt(var + NORM_EPS)


def modulate(h, shift, scale):
    return h * (1 + scale) + shift


def split_cols(t, sizes):
    parts, start = [], 0
    for s in sizes:
        parts.append(t[..., start:start + s])
        start += s
    return parts


def to_heads(t, n_heads):
    b, n, _ = t.shape
    return t.reshape(b, n, n_heads, -1).transpose(0, 2, 1, 3)


def from_heads(t):
    b, h, n, d = t.shape
    return t.transpose(0, 2, 1, 3).reshape(b, n, h * d)


def axial_rope_tables(row, col, rot_dim):
    n_freq = rot_dim // 4
    inv_freq = ROPE_BASE ** (-jnp.arange(n_freq, dtype=F32) / n_freq)
    ang = jnp.concatenate([row[:, None] * inv_freq, col[:, None] * inv_freq], axis=-1)
    return jnp.cos(ang), jnp.sin(ang)


def apply_rope(t, cos, sin):
    tf = t.astype(F32)
    half = tf.shape[-1] // 2
    t1, t2 = tf[..., :half], tf[..., half:]
    return jnp.concatenate([t1 * cos - t2 * sin, t1 * sin + t2 * cos], axis=-1).astype(t.dtype)


def short_conv(u, w):
    pad = CONV_W // 2
    return lax.conv_general_dilated(u, w[:, None, :].astype(u.dtype), window_strides=(1,),
                                    padding=((pad, pad),), dimension_numbers=('NWC', 'WIO', 'NWC'),
                                    feature_group_count=u.shape[-1])


def bidirectional(scan_f, scan_b, args_f, args_b, n_ctx, s0):
    n_tot = args_f[0].shape[2]

    def seg(args, lo, hi, rev):
        parts = [a[:, :, lo:hi] for a in args]
        return [jnp.flip(a, axis=2) for a in parts] if rev else parts

    o_cf, s_f = scan_f(*seg(args_f, 0, n_ctx, False), s0)
    o_cb, s_b = scan_b(*seg(args_b, 0, n_ctx, True), s0)
    o_xf, _ = scan_f(*seg(args_f, n_ctx, n_tot, False), s_f)
    o_xb, _ = scan_b(*seg(args_b, n_ctx, n_tot, True), s_b)
    o_c = o_cf + jnp.flip(o_cb, axis=2)
    o_x = o_xf + jnp.flip(o_xb, axis=2)
    return jnp.concatenate([o_c, o_x], axis=2)


def retention_scan(q, k, v, log_gamma, s0):
    b, h, n, d = q.shape
    nc = n // RET_CHUNK
    pos = jnp.arange(RET_CHUNK, dtype=F32)
    diff = pos[:, None] - pos[None, :]
    intra = jnp.where(diff >= 0, jnp.exp(log_gamma[:, None, None] * jnp.maximum(diff, 0.0)), 0.0)
    q_dec = jnp.exp(log_gamma[:, None] * (pos + 1.0))[..., None]
    k_dec = jnp.exp(log_gamma[:, None] * (RET_CHUNK - 1.0 - pos))[..., None]
    chunk_dec = jnp.exp(log_gamma * RET_CHUNK)[:, None, None]

    def chunks(t):
        return jnp.moveaxis(t.reshape(b, h, nc, RET_CHUNK, d), 2, 0)

    def step(state, inp):
        qi, ki, vi = inp
        scores = jnp.einsum('bhid,bhjd->bhij', qi, ki) * intra
        out = jnp.einsum('bhij,bhjv->bhiv', scores, vi) + jnp.einsum('bhid,bhdv->bhiv', qi * q_dec, state)
        state = state * chunk_dec + jnp.einsum('bhjd,bhjv->bhdv', ki * k_dec, vi)
        return state, out

    s_fin, out = lax.scan(step, s0, (chunks(q), chunks(k), chunks(v)))
    return jnp.moveaxis(out, 0, 2).reshape(b, h, n, d), s_fin


def gdn_scan(q, k, v, g, beta, s0):
    b, h, n, d = q.shape
    nc = n // GDN_CHUNK
    cl = GDN_CHUNK
    q, k, v = (t.reshape(b, h, nc, cl, d) for t in (q, k, v))
    g = jnp.cumsum(g.reshape(b, h, nc, cl), axis=-1)
    beta = beta.reshape(b, h, nc, cl)
    idx = jnp.arange(cl)
    causal = idx[:, None] >= idx[None, :]
    strict = idx[:, None] > idx[None, :]
    diff = g[..., :, None] - g[..., None, :]
    decay = jnp.where(causal, jnp.exp(jnp.where(causal, diff, 0.0)), 0.0)
    k_beta = k * beta[..., None]
    a_mat = jnp.where(strict, jnp.einsum('bhnid,bhnjd->bhnij', k_beta, k) * decay, 0.0)
    eye = jnp.eye(cl, dtype=q.dtype)
    t_mat = lax.linalg.triangular_solve(eye + a_mat, jnp.broadcast_to(eye, a_mat.shape),
                                        left_side=True, lower=True, unit_diagonal=True)
    u = jnp.einsum('bhnij,bhnjd->bhnid', t_mat, v * beta[..., None])
    w = jnp.einsum('bhnij,bhnjd->bhnid', t_mat, k_beta * jnp.exp(g)[..., None])
    attn = jnp.einsum('bhnid,bhnjd->bhnij', q, k) * decay
    q_dec = q * jnp.exp(g)[..., None]
    k_dec = k * jnp.exp(g[..., -1:] - g)[..., None]
    chunk_dec = jnp.exp(g[..., -1])

    def step(state, inp):
        u_i, w_i, qd_i, kd_i, at_i, cd_i = inp
        v_new = u_i - jnp.einsum('bhck,bhkv->bhcv', w_i, state)
        out = jnp.einsum('bhck,bhkv->bhcv', qd_i, state) + jnp.einsum('bhij,bhjv->bhiv', at_i, v_new)
        state = state * cd_i[..., None, None] + jnp.einsum('bhck,bhcv->bhkv', kd_i, v_new)
        return state, out

    xs = tuple(jnp.moveaxis(t, 2, 0) for t in (u, w, q_dec, k_dec, attn, chunk_dec))
    s_fin, out = lax.scan(step, s0, xs)
    return jnp.moveaxis(out, 0, 2).reshape(b, h, n, d), s_fin


def retention_mixer(rq, rk, rv, rg, norm_w, n_ctx, cos, sin):
    q = to_heads(rq.astype(F32), RET_HEADS)
    k = to_heads(rk.astype(F32), RET_HEADS) * HEAD_DIM ** -0.5
    v = to_heads(rv.astype(F32), RET_HEADS)
    q = jnp.concatenate([q[:, :, :n_ctx], apply_rope(q[:, :, n_ctx:], cos, sin)], axis=2)
    k = jnp.concatenate([k[:, :, :n_ctx], apply_rope(k[:, :, n_ctx:], cos, sin)], axis=2)
    log_g_f = jnp.log1p(-jnp.exp2(-5.0 - jnp.arange(RET_HEADS, dtype=F32)))
    log_g_b = jnp.flip(log_g_f)
    s0 = jnp.zeros(q.shape[:2] + (HEAD_DIM, HEAD_DIM), F32)
    o = bidirectional(lambda a, b_, c_, s: retention_scan(a, b_, c_, log_g_f, s),
                      lambda a, b_, c_, s: retention_scan(a, b_, c_, log_g_b, s),
                      (q, k, v), (q, k, v), n_ctx, s0)
    o = from_heads(head_group_norm(o))
    return (o * norm_w.astype(F32) * jax.nn.silu(rg.astype(F32))).astype(rq.dtype)


def gdn_mixer(gq, gk, gv, gz, a_f, a_b, b_f, b_b, conv_w, a_log, dt_bias, norm_w, n_ctx):
    qkv = jnp.concatenate([gq, gk, gv], axis=-1)
    qkv = jnp.concatenate([short_conv(qkv[:, :n_ctx], conv_w), short_conv(qkv[:, n_ctx:], conv_w)], axis=1)
    qkv = jax.nn.silu(qkv.astype(F32))
    q, k, v = (to_heads(t, GDN_HEADS) for t in jnp.split(qkv, 3, axis=-1))
    q = l2_norm(q) * HEAD_DIM ** -0.5
    k = l2_norm(k)

    def gates(a, bb, direction):
        g = -jnp.exp(a_log[direction].astype(F32)) * jax.nn.softplus(a.astype(F32) + dt_bias[direction].astype(F32))
        beta = jax.nn.sigmoid(bb.astype(F32))
        return g.transpose(0, 2, 1), beta.transpose(0, 2, 1)

    g_f, be_f = gates(a_f, b_f, 0)
    g_b, be_b = gates(a_b, b_b, 1)
    s0 = jnp.zeros(q.shape[:2] + (HEAD_DIM, HEAD_DIM), F32)
    o = bidirectional(gdn_scan, gdn_scan, (q, k, v, g_f, be_f), (q, k, v, g_b, be_b), n_ctx, s0)
    o = from_heads(rms_norm(o, norm_w))
    return (o * jax.nn.silu(gz.astype(F32))).astype(gz.dtype)


def softmax_attend(q, k, v):
    s = jnp.einsum('bhqd,bhkd->bhqk', q, k).astype(F32) * MLA_QK ** -0.5
    p = jax.nn.softmax(s, axis=-1).astype(v.dtype)
    return jnp.einsum('bhqk,bhkd->bhqd', p, v)


def mla_mixer(c_q, c_kv, k_rope, cq_norm_w, ckv_norm_w, w_uq, w_ukv, q_norm_w, k_norm_w,
              cos, sin, n_ctx, with_ctx):
    b, n, _ = c_q.shape
    q = matmul(rms_norm(c_q, cq_norm_w).reshape(b * n, Q_LORA), w_uq).reshape(b, n, MLA_HEADS, MLA_QK)
    kv = matmul(rms_norm(c_kv, ckv_norm_w).reshape(b * n, KV_LORA), w_ukv).reshape(
        b, n, MLA_HEADS, NOPE_DIM + V_DIM)
    k = jnp.concatenate([kv[..., :NOPE_DIM],
                         jnp.broadcast_to(k_rope[:, :, None, :], (b, n, MLA_HEADS, ROPE_DIM))], axis=-1)
    q = rms_norm(q, q_norm_w).transpose(0, 2, 1, 3)
    k = rms_norm(k, k_norm_w).transpose(0, 2, 1, 3)
    v = kv[..., NOPE_DIM:].transpose(0, 2, 1, 3)

    def rope_tail(t):
        return jnp.concatenate([t[..., :NOPE_DIM], apply_rope(t[..., NOPE_DIM:], cos, sin)], axis=-1)

    k_c, v_c = k[:, :, :n_ctx], v[:, :, :n_ctx]
    q_x = rope_tail(q[:, :, n_ctx:])
    k_all = jnp.concatenate([rope_tail(k[:, :, n_ctx:]), k_c], axis=2)
    v_all = jnp.concatenate([v[:, :, n_ctx:], v_c], axis=2)
    n_lat = n - n_ctx
    n_blk = n_lat // ATTN_BLOCK
    q_blocks = jnp.moveaxis(q_x.reshape(b, MLA_HEADS, n_blk, ATTN_BLOCK, MLA_QK), 2, 0)
    o_x = lax.map(lambda qb: softmax_attend(qb, k_all, v_all), q_blocks)
    o_x = from_heads(jnp.moveaxis(o_x, 0, 2).reshape(b, MLA_HEADS, n_lat, V_DIM))
    o_c = from_heads(softmax_attend(q[:, :, :n_ctx], k_c, v_c)) if with_ctx else None
    return o_c, o_x


def trunk_layer(x, ctx, mod_x, mod_c, lp, rope_ret, rope_mla, update_ctx):
    b, n_ctx, d = ctx.shape
    n_lat = x.shape[1]
    n_tot = n_ctx + n_lat
    sh1, sc1, g1, sh2, sc2, g2 = (m[:, None, :] for m in jnp.split(mod_x, 6, axis=-1))
    csh1, csc1, cg1, csh2, csc2, cg2 = jnp.split(mod_c, 6, axis=-1)
    h = jnp.concatenate([modulate(rms_norm(ctx, lp['norm1_w']), csh1, csc1),
                         modulate(rms_norm(x, lp['norm1_w']), sh1, sc1)], axis=1)
    h2 = h.reshape(b * n_tot, d).astype(BF16)
    p_ret = matmul(h2, lp['w_in_ret']).reshape(b, n_tot, 4 * RET_W)
    p_gdn = matmul(h2, lp['w_in_gdn']).reshape(b, n_tot, 4 * GDN_W)
    p_sml = matmul(h2, lp['w_in_sml'], tn=2048).reshape(b, n_tot, -1)
    rq, rk, rv, rg = split_cols(p_ret, (RET_W,) * 4)
    gq, gk, gv, gz = split_cols(p_gdn, (GDN_W,) * 4)
    a_f, a_b, b_f, b_b, c_q, c_kv, k_rope = split_cols(
        p_sml, (GDN_HEADS,) * 4 + (Q_LORA, KV_LORA, ROPE_DIM))
    ret_o = retention_mixer(rq, rk, rv, rg, lp['ret_norm_w'], n_ctx, *rope_ret)
    gdn_o = gdn_mixer(gq, gk, gv, gz, a_f, a_b, b_f, b_b, lp['conv_w'], lp['gdn_a_log'],
                      lp['gdn_dt_bias'], lp['gdn_norm_w'], n_ctx)
    mla_c, mla_x = mla_mixer(c_q, c_kv, k_rope, lp['cq_norm_w'], lp['ckv_norm_w'], lp['w_uq'], lp['w_ukv'],
                             lp['q_norm_w'], lp['k_norm_w'], *rope_mla, n_ctx, update_ctx)
    if update_ctx:
        mixed = jnp.concatenate([ret_o, gdn_o, jnp.concatenate([mla_c, mla_x], axis=1)], axis=-1)
        y = matmul(mixed.reshape(b * n_tot, MIX_W).astype(BF16), lp['w_out']).reshape(b, n_tot, d)
        ctx = ctx + cg1 * y[:, :n_ctx]
        x = x + g1 * y[:, n_ctx:]
        f = jnp.concatenate([modulate(rms_norm(ctx, lp['norm2_w']), csh2, csc2),
                             modulate(rms_norm(x, lp['norm2_w']), sh2, sc2)], axis=1)
        f = peer_ffn(f.reshape(b * n_tot, d), lp['peer_wq'], lp['peer_keys'], lp['peer_u'], lp['peer_vt'])
        f = f.reshape(b, n_tot, d)
        ctx = ctx + cg2 * f[:, :n_ctx]
        x = x + g2 * f[:, n_ctx:]
    else:
        mixed = jnp.concatenate([ret_o[:, n_ctx:], gdn_o[:, n_ctx:], mla_x], axis=-1)
        y = matmul(mixed.reshape(b * n_lat, MIX_W).astype(BF16), lp['w_out']).reshape(b, n_lat, d)
        x = x + g1 * y
        f = modulate(rms_norm(x, lp['norm2_w']), sh2, sc2)
        f = peer_ffn(f.reshape(b * n_lat, d), lp['peer_wq'], lp['peer_keys'], lp['peer_u'], lp['peer_vt'])
        x = x + g2 * f.reshape(b, n_lat, d)
    return x, ctx


def kernel(x, c, ctx, c_ctx, ada_w, ada_b, norm1_w, norm2_w, w_in, conv_w, gdn_a_log, gdn_dt_bias,
           gdn_norm_w, ret_norm_w, cq_norm_w, ckv_norm_w, w_uq, w_ukv, q_norm_w, k_norm_w, w_out,
           peer_wq, peer_keys, peer_u, peer_v):
    n_lat = x.shape[1]
    n_rows = n_lat // GRID_W
    row = jnp.repeat(jnp.arange(n_rows, dtype=F32), GRID_W, total_repeat_length=n_lat)
    col = (jnp.arange(n_lat) % GRID_W).astype(F32)
    rope_ret = axial_rope_tables(row, col, HEAD_DIM)
    rope_mla = axial_rope_tables(row, col, ROPE_DIM)
    silu_c = jax.nn.silu(c)
    silu_cc = jax.nn.silu(c_ctx)
    sml_w = 4 * GDN_HEADS + Q_LORA + KV_LORA + ROPE_DIM
    sml_pad = -sml_w % LANES
    for i in range(DEPTH):
        mod_x = silu_c @ ada_w[i] + ada_b[i]
        mod_c = silu_cc @ ada_w[i] + ada_b[i]
        w_in_i = w_in[i].astype(BF16)
        lp = {
            'norm1_w': norm1_w[i], 'norm2_w': norm2_w[i], 'conv_w': conv_w[i],
            'w_in_ret': w_in_i[:, :4 * RET_W],
            'w_in_gdn': w_in_i[:, 4 * RET_W:4 * RET_W + 4 * GDN_W],
            'w_in_sml': jnp.pad(w_in_i[:, 4 * RET_W + 4 * GDN_W:], ((0, 0), (0, sml_pad))),
            'gdn_a_log': gdn_a_log[i], 'gdn_dt_bias': gdn_dt_bias[i], 'gdn_norm_w': gdn_norm_w[i],
            'ret_norm_w': ret_norm_w[i], 'cq_norm_w': cq_norm_w[i], 'ckv_norm_w': ckv_norm_w[i],
            'w_uq': w_uq[i].astype(BF16), 'w_ukv': w_ukv[i].astype(BF16),
            'q_norm_w': q_norm_w[i], 'k_norm_w': k_norm_w[i],
            'w_out': w_out[i].astype(BF16), 'peer_wq': peer_wq[i].astype(BF16), 'peer_keys': peer_keys[i],
            'peer_u': peer_u[i].astype(BF16), 'peer_vt': peer_v[i].astype(BF16).T,
        }
        x, ctx = trunk_layer(x, ctx, mod_x, mod_c, lp, rope_ret, rope_mla, i < DEPTH - 1)
    return x
```

```python
import functools
import math

import jax
import jax.numpy as jnp
from jax import lax
from jax.experimental import pallas as pl
from jax.experimental.pallas import tpu as pltpu

D_MODEL = 4096
DEPTH = 2
GRID_W = 64
HEAD_DIM = 128
RET_HEADS = 8
RET_CHUNK = 128
GDN_HEADS = 8
GDN_CHUNK = 64
CONV_W = 5
MLA_HEADS = 16
Q_LORA = 1024
KV_LORA = 512
NOPE_DIM = 128
ROPE_DIM = 64
V_DIM = 128
MLA_QK = NOPE_DIM + ROPE_DIM
ATTN_BLOCK = 128
PEER_HEADS = 8
N_KEYS = 128
N_EXPERTS = N_KEYS * N_KEYS
PEER_TOPK = 16
PEER_QDIM = 256
ROPE_BASE = 10000.0
NORM_EPS = 1e-6
RET_W = RET_HEADS * HEAD_DIM
GDN_W = GDN_HEADS * HEAD_DIM
MLA_W = MLA_HEADS * V_DIM
MIX_W = RET_W + GDN_W + MLA_W
IN_SIZES = (RET_W,) * 4 + (GDN_W,) * 4 + (GDN_HEADS,) * 4 + (Q_LORA, KV_LORA, ROPE_DIM)

LANES = 128
SUBLANES = 8
VMEM_LIMIT = 56 * 1024 * 1024

F32 = jnp.float32
BF16 = jnp.bfloat16


def _mm_kernel(a_ref, b_ref, o_ref):
    o_ref[...] = jnp.dot(a_ref[...].astype(BF16), b_ref[...].astype(BF16),
                         preferred_element_type=F32).astype(o_ref.dtype)


def _pick_tile(n, target, align):
    best = n
    for t in range(align, min(n, target) + 1, align):
        if n % t == 0:
            best = t
    return best


def matmul(a, b, out_dtype=F32, tm=512, tn=512):
    m, k = a.shape
    n = b.shape[1]
    tm = _pick_tile(m, tm, 8)
    tn = _pick_tile(n, tn, LANES)
    return pl.pallas_call(
        _mm_kernel,
        grid=(m // tm, n // tn),
        in_specs=[pl.BlockSpec((tm, k), lambda i, j: (i, 0)),
                  pl.BlockSpec((k, tn), lambda i, j: (0, j))],
        out_specs=pl.BlockSpec((tm, tn), lambda i, j: (i, j)),
        out_shape=jax.ShapeDtypeStruct((m, n), out_dtype),
        compiler_params=pltpu.CompilerParams(
            dimension_semantics=("parallel", "parallel"), vmem_limit_bytes=VMEM_LIMIT),
    )(a, b)


PEER_TT = 512
PEER_EB = 512
PEER_JB = PEER_EB // N_KEYS
assert SUBLANES == 2 * PEER_JB


def _gelu(a):
    return 0.5 * a * (1.0 + lax.erf(a * (2.0 ** -0.5)))


def _peer_kernel(x_ref, u_ref, vt_ref, s1_ref, e1_ref, s2_ref, e2_ref, tau_ref, o_ref, acc_ref):
    e = pl.program_id(1)

    @pl.when(e == 0)
    def _():
        acc_ref[...] = jnp.zeros_like(acc_ref)

    act_t = lax.dot_general(u_ref[...], x_ref[...], (((1,), (1,)), ((), ())),
                            preferred_element_type=F32)
    upper = (e % 2) == 1

    def key_rows(ref, h, lane):
        blk = ref[h, :, lane]
        return jnp.where(upper, pltpu.roll(blk, PEER_JB, 0), blk)

    rows = []
    for jj in range(PEER_JB):
        cols = []
        for ts in range(PEER_TT // LANES):
            lane = slice(ts * LANES, (ts + 1) * LANES)
            g = jnp.zeros((N_KEYS, LANES), F32)
            for h in range(PEER_HEADS):
                s1row = key_rows(s1_ref, h, lane)[jj:jj + 1, :]
                e1row = key_rows(e1_ref, h, lane)[jj:jj + 1, :]
                t = s2_ref[h, :, lane] + s1row
                g = g + jnp.where(t >= tau_ref[h:h + 1, lane], e2_ref[h, :, lane] * e1row, 0.0)
            a = act_t[jj * N_KEYS:(jj + 1) * N_KEYS, lane]
            cols.append((_gelu(a) * g).astype(BF16))
        rows.append(jnp.concatenate(cols, axis=1))
    c_t = jnp.concatenate(rows, axis=0)
    acc_ref[...] += jnp.dot(vt_ref[...], c_t, preferred_element_type=F32)

    @pl.when(e == pl.num_programs(1) - 1)
    def _():
        o_ref[...] = acc_ref[...].T.astype(o_ref.dtype)


def peer_dense(x, u, vt, s1, e1, s2, e2, tau):
    t, d = x.shape
    n_e = u.shape[0]
    row_spec = pl.BlockSpec((PEER_HEADS, SUBLANES, PEER_TT),
                            lambda i, e: (0, e // (SUBLANES // PEER_JB), i))
    return pl.pallas_call(
        _peer_kernel,
        grid=(t // PEER_TT, n_e // PEER_EB),
        in_specs=[
            pl.BlockSpec((PEER_TT, d), lambda i, e: (i, 0)),
            pl.BlockSpec((PEER_EB, d), lambda i, e: (e, 0)),
            pl.BlockSpec((d, PEER_EB), lambda i, e: (0, e)),
            row_spec,
            row_spec,
            pl.BlockSpec((PEER_HEADS, N_KEYS, PEER_TT), lambda i, e: (0, 0, i)),
            pl.BlockSpec((PEER_HEADS, N_KEYS, PEER_TT), lambda i, e: (0, 0, i)),
            pl.BlockSpec((PEER_HEADS, PEER_TT), lambda i, e: (0, i)),
        ],
        out_specs=pl.BlockSpec((PEER_TT, d), lambda i, e: (i, 0)),
        out_shape=jax.ShapeDtypeStruct((t, d), BF16),
        scratch_shapes=[pltpu.VMEM((d, PEER_TT), F32)],
        compiler_params=pltpu.CompilerParams(
            dimension_semantics=("parallel", "arbitrary"), vmem_limit_bytes=VMEM_LIMIT),
    )(x, u, vt, s1, e1, s2, e2, tau)


def _extract_top(work_ref, top_ref, n_groups):
    n_rows = work_ref.shape[1]
    iota = lax.broadcasted_iota(jnp.int32, (n_rows, LANES), 0)

    def body(r, carry):
        for g in range(n_groups):
            w = work_ref[g]
            m = jnp.max(w, axis=0, keepdims=True)
            top_ref[g, pl.ds(r, 1), :] = m
            first = jnp.min(jnp.where(w == m, iota, n_rows), axis=0, keepdims=True)
            work_ref[g] = jnp.where(iota == first, -jnp.inf, w)
        return carry

    lax.fori_loop(0, PEER_TOPK, body, 0)


def _peer_select_kernel(q_ref, keys_ref, s1_ref, e1_ref, s2_ref, e2_ref, tau_ref,
                        work_ref, top_ref, cand_ref, cwork_ref, ctop_ref):
    for h in range(PEER_HEADS):
        for p in range(2):
            g = 2 * h + p
            sc = lax.dot_general(keys_ref[h, p], q_ref[:, g * N_KEYS:(g + 1) * N_KEYS],
                                 (((1,), (1,)), ((), ())), preferred_element_type=F32,
                                 precision=lax.Precision.HIGHEST)
            work_ref[g] = sc
            (s1_ref, s2_ref)[p][h] = sc
    _extract_top(work_ref, top_ref, 2 * PEER_HEADS)
    for h in range(PEER_HEADS):
        t2 = top_ref[2 * h + 1]
        for p in range(PEER_TOPK):
            c = top_ref[2 * h, p:p + 1, :] + t2
            cand_ref[h, p * PEER_TOPK:(p + 1) * PEER_TOPK, :] = c
            cwork_ref[h, p * PEER_TOPK:(p + 1) * PEER_TOPK, :] = c
    _extract_top(cwork_ref, ctop_ref, PEER_HEADS)
    for h in range(PEER_HEADS):
        tau = ctop_ref[h, PEER_TOPK - 1:PEER_TOPK, :]
        m1 = top_ref[2 * h, 0:1, :]
        m2 = top_ref[2 * h + 1, 0:1, :]
        cand = cand_ref[h]
        z = jnp.sum(jnp.where(cand >= tau, jnp.exp(cand - (m1 + m2)), 0.0), axis=0, keepdims=True)
        tau_ref[h:h + 1, :] = tau
        e1_ref[h] = jnp.exp(s1_ref[h] - m1)
        e2_ref[h] = jnp.exp(s2_ref[h] - m2) / z


def peer_select(q, keys):
    t = q.shape[0]
    big = jax.ShapeDtypeStruct((PEER_HEADS, N_KEYS, t), F32)
    big_spec = pl.BlockSpec((PEER_HEADS, N_KEYS, LANES), lambda i: (0, 0, i))
    n_cand = PEER_TOPK * PEER_TOPK
    return pl.pallas_call(
        _peer_select_kernel,
        grid=(t // LANES,),
        in_specs=[pl.BlockSpec((LANES, q.shape[1]), lambda i: (i, 0)),
                  pl.BlockSpec(keys.shape, lambda i: (0, 0, 0, 0))],
        out_specs=[big_spec, big_spec, big_spec, big_spec,
                   pl.BlockSpec((PEER_HEADS, LANES), lambda i: (0, i))],
        out_shape=[big, big, big, big, jax.ShapeDtypeStruct((PEER_HEADS, t), F32)],
        scratch_shapes=[pltpu.VMEM((2 * PEER_HEADS, N_KEYS, LANES), F32),
                        pltpu.VMEM((2 * PEER_HEADS, PEER_TOPK, LANES), F32),
                        pltpu.VMEM((PEER_HEADS, n_cand, LANES), F32),
                        pltpu.VMEM((PEER_HEADS, n_cand, LANES), F32),
                        pltpu.VMEM((PEER_HEADS, PEER_TOPK, LANES), F32)],
        compiler_params=pltpu.CompilerParams(
            dimension_semantics=("parallel",), vmem_limit_bytes=VMEM_LIMIT),
    )(q, keys)


def peer_ffn(f_bf, wq, sub_keys, u_bf, vt_bf):
    q = matmul(f_bf, wq, F32)
    s1, e1, s2, e2, tau = peer_select(q, sub_keys)
    return peer_dense(f_bf, u_bf, vt_bf, s1, e1, s2, e2, tau)


def rms_norm(x, w):
    xf = x.astype(F32)
    y = xf * lax.rsqrt(jnp.mean(xf * xf, axis=-1, keepdims=True) + NORM_EPS)
    return (y * w.astype(F32)).astype(x.dtype)


def l2_norm(t):
    return t * lax.rsqrt(jnp.sum(t * t, axis=-1, keepdims=True) + NORM_EPS)


def head_group_norm(o):
    mu = jnp.mean(o, axis=-1, keepdims=True)
    var = jnp.mean(jnp.square(o - mu), axis=-1, keepdims=True)
    return (o - mu) * lax.rsqrt(var + NORM_EPS)


def modulate(h, shift, scale):
    return h * (1 + scale) + shift


def split_cols(t, sizes):
    parts, start = [], 0
    for s in sizes:
        parts.append(t[..., start:start + s])
        start += s
    return parts


def to_heads(t, n_heads):
    b, n, _ = t.shape
    return t.reshape(b, n, n_heads, -1).transpose(0, 2, 1, 3)


def from_heads(t):
    b, h, n, d = t.shape
    return t.transpose(0, 2, 1, 3).reshape(b, n, h * d)


def axial_rope_tables(row, col, rot_dim):
    n_freq = rot_dim // 4
    inv_freq = ROPE_BASE ** (-jnp.arange(n_freq, dtype=F32) / n_freq)
    ang = jnp.concatenate([row[:, None] * inv_freq, col[:, None] * inv_freq], axis=-1)
    return jnp.cos(ang), jnp.sin(ang)


def apply_rope(t, cos, sin):
    tf = t.astype(F32)
    half = tf.shape[-1] // 2
    t1, t2 = tf[..., :half], tf[..., half:]
    return jnp.concatenate([t1 * cos - t2 * sin, t1 * sin + t2 * cos], axis=-1).astype(t.dtype)


def short_conv(u, w):
    pad = CONV_W // 2
    return lax.conv_general_dilated(u, w[:, None, :].astype(u.dtype), window_strides=(1,),
                                    padding=((pad, pad),), dimension_numbers=('NWC', 'WIO', 'NWC'),
                                    feature_group_count=u.shape[-1])


def bidirectional(scan_f, scan_b, args_f, args_b, n_ctx, s0):
    n_tot = args_f[0].shape[2]

    def seg(args, lo, hi, rev):
        parts = [a[:, :, lo:hi] for a in args]
        return [jnp.flip(a, axis=2) for a in parts] if rev else parts

    o_cf, s_f = scan_f(*seg(args_f, 0, n_ctx, False), s0)
    o_cb, s_b = scan_b(*seg(args_b, 0, n_ctx, True), s0)
    o_xf, _ = scan_f(*seg(args_f, n_ctx, n_tot, False), s_f)
    o_xb, _ = scan_b(*seg(args_b, n_ctx, n_tot, True), s_b)
    o_c = o_cf + jnp.flip(o_cb, axis=2)
    o_x = o_xf + jnp.flip(o_xb, axis=2)
    return jnp.concatenate([o_c, o_x], axis=2)


def retention_scan(q, k, v, log_gamma, s0):
    b, h, n, d = q.shape
    nc = n // RET_CHUNK
    pos = jnp.arange(RET_CHUNK, dtype=F32)
    diff = pos[:, None] - pos[None, :]
    intra = jnp.where(diff >= 0, jnp.exp(log_gamma[:, None, None] * jnp.maximum(diff, 0.0)), 0.0)
    q_dec = jnp.exp(log_gamma[:, None] * (pos + 1.0))[..., None]
    k_dec = jnp.exp(log_gamma[:, None] * (RET_CHUNK - 1.0 - pos))[..., None]
    chunk_dec = jnp.exp(log_gamma * RET_CHUNK)[:, None, None]

    def chunks(t):
        return jnp.moveaxis(t.reshape(b, h, nc, RET_CHUNK, d), 2, 0)

    def step(state, inp):
        qi, ki, vi = inp
        scores = jnp.einsum('bhid,bhjd->bhij', qi, ki) * intra
        out = jnp.einsum('bhij,bhjv->bhiv', scores, vi) + jnp.einsum('bhid,bhdv->bhiv', qi * q_dec, state)
        state = state * chunk_dec + jnp.einsum('bhjd,bhjv->bhdv', ki * k_dec, vi)
        return state, out

    s_fin, out = lax.scan(step, s0, (chunks(q), chunks(k), chunks(v)))
    return jnp.moveaxis(out, 0, 2).reshape(b, h, n, d), s_fin


def gdn_scan(q, k, v, g, beta, s0):
    b, h, n, d = q.shape
    nc = n // GDN_CHUNK
    cl = GDN_CHUNK
    q, k, v = (t.reshape(b, h, nc, cl, d) for t in (q, k, v))
    g = jnp.cumsum(g.reshape(b, h, nc, cl), axis=-1)
    beta = beta.reshape(b, h, nc, cl)
    idx = jnp.arange(cl)
    causal = idx[:, None] >= idx[None, :]
    strict = idx[:, None] > idx[None, :]
    diff = g[..., :, None] - g[..., None, :]
    decay = jnp.where(causal, jnp.exp(jnp.where(causal, diff, 0.0)), 0.0)
    k_beta = k * beta[..., None]
    a_mat = jnp.where(strict, jnp.einsum('bhnid,bhnjd->bhnij', k_beta, k) * decay, 0.0)
    eye = jnp.eye(cl, dtype=q.dtype)
    t_mat = lax.linalg.triangular_solve(eye + a_mat, jnp.broadcast_to(eye, a_mat.shape),
                                        left_side=True, lower=True, unit_diagonal=True)
    u = jnp.einsum('bhnij,bhnjd->bhnid', t_mat, v * beta[..., None])
    w = jnp.einsum('bhnij,bhnjd->bhnid', t_mat, k_beta * jnp.exp(g)[..., None])
    attn = jnp.einsum('bhnid,bhnjd->bhnij', q, k) * decay
    q_dec = q * jnp.exp(g)[..., None]
    k_dec = k * jnp.exp(g[..., -1:] - g)[..., None]
    chunk_dec = jnp.exp(g[..., -1])

    def step(state, inp):
        u_i, w_i, qd_i, kd_i, at_i, cd_i = inp
        v_new = u_i - jnp.einsum('bhck,bhkv->bhcv', w_i, state)
        out = jnp.einsum('bhck,bhkv->bhcv', qd_i, state) + jnp.einsum('bhij,bhjv->bhiv', at_i, v_new)
        state = state * cd_i[..., None, None] + jnp.einsum('bhck,bhcv->bhkv', kd_i, v_new)
        return state, out

    xs = tuple(jnp.moveaxis(t, 2, 0) for t in (u, w, q_dec, k_dec, attn, chunk_dec))
    s_fin, out = lax.scan(step, s0, xs)
    return jnp.moveaxis(out, 0, 2).reshape(b, h, n, d), s_fin


def retention_mixer(rq, rk, rv, rg, norm_w, n_ctx, cos, sin):
    q = to_heads(rq.astype(F32), RET_HEADS)
    k = to_heads(rk.astype(F32), RET_HEADS) * HEAD_DIM ** -0.5
    v = to_heads(rv.astype(F32), RET_HEADS)
    q = jnp.concatenate([q[:, :, :n_ctx], apply_rope(q[:, :, n_ctx:], cos, sin)], axis=2)
    k = jnp.concatenate([k[:, :, :n_ctx], apply_rope(k[:, :, n_ctx:], cos, sin)], axis=2)
    log_g_f = jnp.log1p(-jnp.exp2(-5.0 - jnp.arange(RET_HEADS, dtype=F32)))
    log_g_b = jnp.flip(log_g_f)
    s0 = jnp.zeros(q.shape[:2] + (HEAD_DIM, HEAD_DIM), F32)
    o = bidirectional(lambda a, b_, c_, s: retention_scan(a, b_, c_, log_g_f, s),
                      lambda a, b_, c_, s: retention_scan(a, b_, c_, log_g_b, s),
                      (q, k, v), (q, k, v), n_ctx, s0)
    o = from_heads(head_group_norm(o))
    return (o * norm_w.astype(F32) * jax.nn.silu(rg.astype(F32))).astype(rq.dtype)


def gdn_mixer(gq, gk, gv, gz, a_f, a_b, b_f, b_b, conv_w, a_log, dt_bias, norm_w, n_ctx):
    qkv = jnp.concatenate([gq, gk, gv], axis=-1)
    qkv = jnp.concatenate([short_conv(qkv[:, :n_ctx], conv_w), short_conv(qkv[:, n_ctx:], conv_w)], axis=1)
    qkv = jax.nn.silu(qkv.astype(F32))
    q, k, v = (to_heads(t, GDN_HEADS) for t in jnp.split(qkv, 3, axis=-1))
    q = l2_norm(q) * HEAD_DIM ** -0.5
    k = l2_norm(k)

    def gates(a, bb, direction):
        g = -jnp.exp(a_log[direction].astype(F32)) * jax.nn.softplus(a.astype(F32) + dt_bias[direction].astype(F32))
        beta = jax.nn.sigmoid(bb.astype(F32))
        return g.transpose(0, 2, 1), beta.transpose(0, 2, 1)

    g_f, be_f = gates(a_f, b_f, 0)
    g_b, be_b = gates(a_b, b_b, 1)
    s0 = jnp.zeros(q.shape[:2] + (HEAD_DIM, HEAD_DIM), F32)
    o = bidirectional(gdn_scan, gdn_scan, (q, k, v, g_f, be_f), (q, k, v, g_b, be_b), n_ctx, s0)
    o = from_heads(rms_norm(o, norm_w))
    return (o * jax.nn.silu(gz.astype(F32))).astype(gz.dtype)


MLA_HP = 2 * LANES
MLA_TT = 256
MLA_TQ = 512
MLA_TK = 512


def _rope_tail(tail, c, s1, s2):
    half = ROPE_DIM // 2
    return tail * c + pltpu.roll(tail, LANES - half, 1) * s1 + pltpu.roll(tail, half, 1) * s2


def _mla_q_prep_kernel(x_ref, w_ref, c_ref, s1_ref, s2_ref, o_ref):
    x = x_ref[...]
    ms = jnp.sum(x * x, axis=1, keepdims=True) * (1.0 / MLA_QK)
    y = x * lax.rsqrt(ms + NORM_EPS) * w_ref[...]
    tail = _rope_tail(y[:, NOPE_DIM:], c_ref[...], s1_ref[...], s2_ref[...])
    o_ref[:, :NOPE_DIM] = (y[:, :NOPE_DIM] * MLA_QK ** -0.5).astype(o_ref.dtype)
    o_ref[:, NOPE_DIM:] = (tail * MLA_QK ** -0.5).astype(o_ref.dtype)


def _mla_kv_prep_kernel(kv_ref, kr_ref, w_ref, c_ref, s1_ref, s2_ref, k_ref, v_ref):
    nope = kv_ref[:, :NOPE_DIM]
    kr = kr_ref[...]
    ms = (jnp.sum(nope * nope, axis=1, keepdims=True)
          + jnp.sum(kr * kr, axis=1, keepdims=True)) * (1.0 / MLA_QK)
    r = lax.rsqrt(ms + NORM_EPS)
    k_ref[:, :NOPE_DIM] = (nope * r * w_ref[:, :NOPE_DIM]).astype(k_ref.dtype)
    tail = _rope_tail(kr * r * w_ref[:, NOPE_DIM:], c_ref[...], s1_ref[...], s2_ref[...])
    k_ref[:, NOPE_DIM:] = tail.astype(k_ref.dtype)
    v_ref[...] = kv_ref[:, NOPE_DIM:].astype(v_ref.dtype)


def mla_prep(qraw, kvraw, kr_pad, qw_pad, kw_pad, tabs, n_tot):
    t = qraw.shape[0]
    tiles = n_tot // MLA_TT
    tab_spec = pl.BlockSpec((MLA_TT, LANES), lambda i, h: (i % tiles, 0))
    w_spec = pl.BlockSpec((1, MLA_HP), lambda i, h: (0, 0))
    head_spec = pl.BlockSpec((MLA_TT, MLA_HP), lambda i, h: (i, h))
    params = pltpu.CompilerParams(dimension_semantics=("parallel", "parallel"), vmem_limit_bytes=VMEM_LIMIT)
    q = pl.pallas_call(
        _mla_q_prep_kernel,
        grid=(t // MLA_TT, MLA_HEADS),
        in_specs=[head_spec, w_spec, tab_spec, tab_spec, tab_spec],
        out_specs=head_spec,
        out_shape=jax.ShapeDtypeStruct((t, MLA_HEADS * MLA_HP), BF16),
        compiler_params=params,
    )(qraw, qw_pad, *tabs)
    k, v = pl.pallas_call(
        _mla_kv_prep_kernel,
        grid=(t // MLA_TT, MLA_HEADS),
        in_specs=[head_spec, pl.BlockSpec((MLA_TT, LANES), lambda i, h: (i, 0)), w_spec,
                  tab_spec, tab_spec, tab_spec],
        out_specs=[head_spec, pl.BlockSpec((MLA_TT, V_DIM), lambda i, h: (i, h))],
        out_shape=[jax.ShapeDtypeStruct((t, MLA_HEADS * MLA_HP), BF16),
                   jax.ShapeDtypeStruct((t, MLA_HEADS * V_DIM), BF16)],
        compiler_params=params,
    )(kvraw, kr_pad, kw_pad, *tabs)
    return q, k, v


def _mla_flash_kernel(q_ref, k_ref, v_ref, o_ref, *, n_loop, tail_len):
    q = q_ref[0]
    tq = q.shape[0]

    def attend(kc, vc, carry):
        m, l, acc = carry
        s = lax.dot_general(q, kc, (((1,), (1,)), ((), ())), preferred_element_type=F32)
        m_new = jnp.maximum(m, jnp.max(s, axis=1, keepdims=True))
        alpha = jnp.exp(m - m_new)
        p = jnp.exp(s - m_new)
        l = alpha * l + jnp.sum(p, axis=1, keepdims=True)
        acc = alpha * acc + jnp.dot(p.astype(vc.dtype), vc, preferred_element_type=F32)
        return m_new, l, acc

    def body(c, carry):
        off = pl.multiple_of(c * MLA_TK, MLA_TK)
        return attend(k_ref[0, pl.ds(off, MLA_TK), :], v_ref[0, pl.ds(off, MLA_TK), :], carry)

    carry = (jnp.full((tq, 1), -jnp.inf, F32), jnp.zeros((tq, 1), F32), jnp.zeros((tq, V_DIM), F32))
    if n_loop:
        carry = lax.fori_loop(0, n_loop, body, carry)
    if tail_len:
        lo = n_loop * MLA_TK
        carry = attend(k_ref[0, lo:lo + tail_len, :], v_ref[0, lo:lo + tail_len, :], carry)
    _, l, acc = carry
    o_ref[0] = (acc / l).astype(o_ref.dtype)


def mla_attend(q, k, v, n_lat, n_ctx, context_queries):
    b = q.shape[0]
    params = pltpu.CompilerParams(dimension_semantics=("parallel", "parallel", "arbitrary"),
                                  vmem_limit_bytes=VMEM_LIMIT)
    if context_queries:
        blk = n_lat // n_ctx
        return pl.pallas_call(
            functools.partial(_mla_flash_kernel, n_loop=0, tail_len=n_ctx),
            grid=(b, MLA_HEADS, 1),
            in_specs=[pl.BlockSpec((1, n_ctx, MLA_HP), lambda bi, h, i: (bi, blk, h)),
                      pl.BlockSpec((1, n_ctx, MLA_HP), lambda bi, h, i: (bi, blk, h)),
                      pl.BlockSpec((1, n_ctx, V_DIM), lambda bi, h, i: (bi, blk, h))],
            out_specs=pl.BlockSpec((1, n_ctx, V_DIM), lambda bi, h, i: (bi, 0, h)),
            out_shape=jax.ShapeDtypeStruct((b, n_ctx, MLA_HEADS * V_DIM), BF16),
            compiler_params=params,
        )(q, k, v)
    n_tot = n_lat + n_ctx
    return pl.pallas_call(
        functools.partial(_mla_flash_kernel, n_loop=n_lat // MLA_TK, tail_len=n_ctx),
        grid=(b, MLA_HEADS, n_lat // MLA_TQ),
        in_specs=[pl.BlockSpec((1, MLA_TQ, MLA_HP), lambda bi, h, i: (bi, i, h)),
                  pl.BlockSpec((1, n_tot, MLA_HP), lambda bi, h, i: (bi, 0, h)),
                  pl.BlockSpec((1, n_tot, V_DIM), lambda bi, h, i: (bi, 0, h))],
        out_specs=pl.BlockSpec((1, MLA_TQ, V_DIM), lambda bi, h, i: (bi, i, h)),
        out_shape=jax.ShapeDtypeStruct((b, n_lat, MLA_HEADS * V_DIM), BF16),
        compiler_params=params,
    )(q, k, v)


def mla_tables(cos, sin, n_ctx):
    n_lat, half = cos.shape
    z = jnp.zeros((n_lat, LANES - 2 * half), F32)
    zh = jnp.zeros((n_lat, half), F32)
    c = jnp.concatenate([cos, cos, z], axis=1)
    s1 = jnp.concatenate([-sin, zh, z], axis=1)
    s2 = jnp.concatenate([zh, sin, z], axis=1)
    ident = jnp.concatenate([jnp.ones((n_ctx, 2 * half), F32), jnp.zeros((n_ctx, LANES - 2 * half), F32)], axis=1)
    zero = jnp.zeros((n_ctx, LANES), F32)
    return (jnp.concatenate([c, ident], axis=0), jnp.concatenate([s1, zero], axis=0),
            jnp.concatenate([s2, zero], axis=0))


def mla_mixer(c_q, c_kv, k_rope, lp, tabs, n_lat, with_ctx):
    b, n, _ = c_q.shape
    n_ctx = n - n_lat
    qraw = matmul(rms_norm(c_q, lp['cq_norm_w']).reshape(b * n, Q_LORA).astype(BF16), lp['w_uq_pad'])
    kvraw = matmul(rms_norm(c_kv, lp['ckv_norm_w']).reshape(b * n, KV_LORA).astype(BF16), lp['w_ukv'])
    kr_pad = jnp.pad(k_rope.reshape(b * n, ROPE_DIM), ((0, 0), (0, LANES - ROPE_DIM)))
    q, k, v = mla_prep(qraw, kvraw, kr_pad, lp['q_norm_pad'], lp['k_norm_pad'], tabs, n)
    q = q.reshape(b, n, -1)
    k = k.reshape(b, n, -1)
    v = v.reshape(b, n, -1)
    o_x = mla_attend(q, k, v, n_lat, n_ctx, False)
    if not with_ctx:
        return o_x
    return jnp.concatenate([o_x, mla_attend(q, k, v, n_lat, n_ctx, True)], axis=1)


def trunk_layer(x, ctx, mod_x, mod_c, lp, rope_ret, mla_tabs, update_ctx):
    b, n_ctx, d = ctx.shape
    n_lat = x.shape[1]
    n_tot = n_ctx + n_lat
    sh1, sc1, g1, sh2, sc2, g2 = (m[:, None, :] for m in jnp.split(mod_x, 6, axis=-1))
    csh1, csc1, cg1, csh2, csc2, cg2 = jnp.split(mod_c, 6, axis=-1)

    def ctx_first(t):
        return jnp.concatenate([t[:, n_lat:], t[:, :n_lat]], axis=1)

    def ctx_last(t):
        return jnp.concatenate([t[:, n_ctx:], t[:, :n_ctx]], axis=1)

    h = jnp.concatenate([modulate(rms_norm(x, lp['norm1_w']), sh1, sc1),
                         modulate(rms_norm(ctx, lp['norm1_w']), csh1, csc1)], axis=1)
    h2 = h.reshape(b * n_tot, d).astype(BF16)
    p_ret = matmul(h2, lp['w_in_ret']).reshape(b, n_tot, 4 * RET_W)
    p_gdn = matmul(h2, lp['w_in_gdn']).reshape(b, n_tot, 4 * GDN_W)
    p_sml = matmul(h2, lp['w_in_sml'], tn=2048).reshape(b, n_tot, -1)
    rq, rk, rv, rg = split_cols(p_ret, (RET_W,) * 4)
    gq, gk, gv, gz = split_cols(p_gdn, (GDN_W,) * 4)
    a_f, a_b, b_f, b_b, c_q, c_kv, k_rope = split_cols(
        p_sml, (GDN_HEADS,) * 4 + (Q_LORA, KV_LORA, ROPE_DIM))
    ret_o = ctx_last(retention_mixer(*(ctx_first(t) for t in (rq, rk, rv, rg)), lp['ret_norm_w'], n_ctx,
                                     *rope_ret))
    gdn_o = ctx_last(gdn_mixer(*(ctx_first(t) for t in (gq, gk, gv, gz, a_f, a_b, b_f, b_b)), lp['conv_w'],
                               lp['gdn_a_log'], lp['gdn_dt_bias'], lp['gdn_norm_w'], n_ctx))
    mla_o = mla_mixer(c_q, c_kv, k_rope, lp, mla_tabs, n_lat, update_ctx)
    if update_ctx:
        mixed = jnp.concatenate([ret_o.astype(BF16), gdn_o.astype(BF16), mla_o], axis=-1)
        y = matmul(mixed.reshape(b * n_tot, MIX_W), lp['w_out']).reshape(b, n_tot, d)
        x = x + g1 * y[:, :n_lat]
        ctx = ctx + cg1 * y[:, n_lat:]
        f = jnp.concatenate([modulate(rms_norm(x, lp['norm2_w']), sh2, sc2),
                             modulate(rms_norm(ctx, lp['norm2_w']), csh2, csc2)], axis=1)
        f = peer_ffn(f.reshape(b * n_tot, d).astype(BF16), lp['peer_wq'], lp['peer_keys'], lp['peer_u'],
                     lp['peer_vt']).reshape(b, n_tot, d)
        x = x + g2 * f[:, :n_lat]
        ctx = ctx + cg2 * f[:, n_lat:]
    else:
        mixed = jnp.concatenate([ret_o[:, :n_lat].astype(BF16), gdn_o[:, :n_lat].astype(BF16), mla_o], axis=-1)
        y = matmul(mixed.reshape(b * n_lat, MIX_W), lp['w_out']).reshape(b, n_lat, d)
        x = x + g1 * y
        f = modulate(rms_norm(x, lp['norm2_w']), sh2, sc2)
        f = peer_ffn(f.reshape(b * n_lat, d).astype(BF16), lp['peer_wq'], lp['peer_keys'], lp['peer_u'],
                     lp['peer_vt'])
        x = x + g2 * f.reshape(b, n_lat, d)
    return x, ctx


def kernel(x, c, ctx, c_ctx, ada_w, ada_b, norm1_w, norm2_w, w_in, conv_w, gdn_a_log, gdn_dt_bias,
           gdn_norm_w, ret_norm_w, cq_norm_w, ckv_norm_w, w_uq, w_ukv, q_norm_w, k_norm_w, w_out,
           peer_wq, peer_keys, peer_u, peer_v):
    n_lat = x.shape[1]
    n_rows = n_lat // GRID_W
    row = jnp.repeat(jnp.arange(n_rows, dtype=F32), GRID_W, total_repeat_length=n_lat)
    col = (jnp.arange(n_lat) % GRID_W).astype(F32)
    rope_ret = axial_rope_tables(row, col, HEAD_DIM)
    mla_tabs = mla_tables(*axial_rope_tables(row, col, ROPE_DIM), ctx.shape[1])
    silu_c = jax.nn.silu(c)
    silu_cc = jax.nn.silu(c_ctx)
    sml_w = 4 * GDN_HEADS + Q_LORA + KV_LORA + ROPE_DIM
    sml_pad = -sml_w % LANES
    for i in range(DEPTH):
        mod_x = silu_c @ ada_w[i] + ada_b[i]
        mod_c = silu_cc @ ada_w[i] + ada_b[i]
        w_in_i = w_in[i].astype(BF16)
        lp = {
            'norm1_w': norm1_w[i], 'norm2_w': norm2_w[i], 'conv_w': conv_w[i],
            'w_in_ret': w_in_i[:, :4 * RET_W],
            'w_in_gdn': w_in_i[:, 4 * RET_W:4 * RET_W + 4 * GDN_W],
            'w_in_sml': jnp.pad(w_in_i[:, 4 * RET_W + 4 * GDN_W:], ((0, 0), (0, sml_pad))),
            'gdn_a_log': gdn_a_log[i], 'gdn_dt_bias': gdn_dt_bias[i], 'gdn_norm_w': gdn_norm_w[i],
            'ret_norm_w': ret_norm_w[i], 'cq_norm_w': cq_norm_w[i], 'ckv_norm_w': ckv_norm_w[i],
            'w_uq_pad': jnp.pad(w_uq[i].astype(BF16).reshape(Q_LORA, MLA_HEADS, MLA_QK),
                                ((0, 0), (0, 0), (0, MLA_HP - MLA_QK))).reshape(Q_LORA, MLA_HEADS * MLA_HP),
            'w_ukv': w_ukv[i].astype(BF16),
            'q_norm_pad': jnp.pad(q_norm_w[i], (0, MLA_HP - MLA_QK))[None, :],
            'k_norm_pad': jnp.pad(k_norm_w[i], (0, MLA_HP - MLA_QK))[None, :],
            'w_out': w_out[i].astype(BF16), 'peer_wq': peer_wq[i].astype(BF16), 'peer_keys': peer_keys[i],
            'peer_u': peer_u[i].astype(BF16), 'peer_vt': peer_v[i].astype(BF16).T,
        }
        x, ctx = trunk_layer(x, ctx, mod_x, mod_c, lp, rope_ret, mla_tabs, i < DEPTH - 1)
    return x
```

```python
import functools
import math

import jax
import jax.numpy as jnp
import numpy as np
from jax import lax
from jax.experimental import pallas as pl
from jax.experimental.pallas import tpu as pltpu

D_MODEL = 4096
DEPTH = 2
GRID_W = 64
HEAD_DIM = 128
RET_HEADS = 8
RET_CHUNK = 128
GDN_HEADS = 8
GDN_CHUNK = 64
CONV_W = 5
MLA_HEADS = 16
Q_LORA = 1024
KV_LORA = 512
NOPE_DIM = 128
ROPE_DIM = 64
V_DIM = 128
MLA_QK = NOPE_DIM + ROPE_DIM
ATTN_BLOCK = 128
PEER_HEADS = 8
N_KEYS = 128
N_EXPERTS = N_KEYS * N_KEYS
PEER_TOPK = 16
PEER_QDIM = 256
ROPE_BASE = 10000.0
NORM_EPS = 1e-6
RET_W = RET_HEADS * HEAD_DIM
GDN_W = GDN_HEADS * HEAD_DIM
MLA_W = MLA_HEADS * V_DIM
MIX_W = RET_W + GDN_W + MLA_W
IN_SIZES = (RET_W,) * 4 + (GDN_W,) * 4 + (GDN_HEADS,) * 4 + (Q_LORA, KV_LORA, ROPE_DIM)

LANES = 128
SUBLANES = 8
VMEM_LIMIT = 56 * 1024 * 1024

F32 = jnp.float32
BF16 = jnp.bfloat16


def _mm_kernel(a_ref, b_ref, o_ref):
    o_ref[...] = jnp.dot(a_ref[...].astype(BF16), b_ref[...].astype(BF16),
                         preferred_element_type=F32).astype(o_ref.dtype)


def _pick_tile(n, target, align):
    best = n
    for t in range(align, min(n, target) + 1, align):
        if n % t == 0:
            best = t
    return best


def matmul(a, b, out_dtype=F32, tm=512, tn=512):
    m, k = a.shape
    n = b.shape[1]
    tm = _pick_tile(m, tm, 8)
    tn = _pick_tile(n, tn, LANES)
    return pl.pallas_call(
        _mm_kernel,
        grid=(m // tm, n // tn),
        in_specs=[pl.BlockSpec((tm, k), lambda i, j: (i, 0)),
                  pl.BlockSpec((k, tn), lambda i, j: (0, j))],
        out_specs=pl.BlockSpec((tm, tn), lambda i, j: (i, j)),
        out_shape=jax.ShapeDtypeStruct((m, n), out_dtype),
        compiler_params=pltpu.CompilerParams(
            dimension_semantics=("parallel", "parallel"), vmem_limit_bytes=VMEM_LIMIT),
    )(a, b)


PEER_TT = 512
PEER_EB = 512
PEER_JB = PEER_EB // N_KEYS
assert SUBLANES == 2 * PEER_JB


def _gelu(a):
    return 0.5 * a * (1.0 + lax.erf(a * (2.0 ** -0.5)))


def _peer_kernel(x_ref, u_ref, vt_ref, s1_ref, e1_ref, s2_ref, e2_ref, tau_ref, o_ref, acc_ref):
    e = pl.program_id(1)

    @pl.when(e == 0)
    def _():
        acc_ref[...] = jnp.zeros_like(acc_ref)

    act_t = lax.dot_general(u_ref[...], x_ref[...], (((1,), (1,)), ((), ())),
                            preferred_element_type=F32)
    upper = (e % 2) == 1

    def key_rows(ref, h, lane):
        blk = ref[h, :, lane]
        return jnp.where(upper, pltpu.roll(blk, PEER_JB, 0), blk)

    rows = []
    for jj in range(PEER_JB):
        cols = []
        for ts in range(PEER_TT // LANES):
            lane = slice(ts * LANES, (ts + 1) * LANES)
            g = jnp.zeros((N_KEYS, LANES), F32)
            for h in range(PEER_HEADS):
                s1row = key_rows(s1_ref, h, lane)[jj:jj + 1, :]
                e1row = key_rows(e1_ref, h, lane)[jj:jj + 1, :]
                t = s2_ref[h, :, lane] + s1row
                g = g + jnp.where(t >= tau_ref[h:h + 1, lane], e2_ref[h, :, lane] * e1row, 0.0)
            a = act_t[jj * N_KEYS:(jj + 1) * N_KEYS, lane]
            cols.append((_gelu(a) * g).astype(BF16))
        rows.append(jnp.concatenate(cols, axis=1))
    c_t = jnp.concatenate(rows, axis=0)
    acc_ref[...] += jnp.dot(vt_ref[...], c_t, preferred_element_type=F32)

    @pl.when(e == pl.num_programs(1) - 1)
    def _():
        o_ref[...] = acc_ref[...].T.astype(o_ref.dtype)


def peer_dense(x, u, vt, s1, e1, s2, e2, tau):
    t, d = x.shape
    n_e = u.shape[0]
    row_spec = pl.BlockSpec((PEER_HEADS, SUBLANES, PEER_TT),
                            lambda i, e: (0, e // (SUBLANES // PEER_JB), i))
    return pl.pallas_call(
        _peer_kernel,
        grid=(t // PEER_TT, n_e // PEER_EB),
        in_specs=[
            pl.BlockSpec((PEER_TT, d), lambda i, e: (i, 0)),
            pl.BlockSpec((PEER_EB, d), lambda i, e: (e, 0)),
            pl.BlockSpec((d, PEER_EB), lambda i, e: (0, e)),
            row_spec,
            row_spec,
            pl.BlockSpec((PEER_HEADS, N_KEYS, PEER_TT), lambda i, e: (0, 0, i)),
            pl.BlockSpec((PEER_HEADS, N_KEYS, PEER_TT), lambda i, e: (0, 0, i)),
            pl.BlockSpec((PEER_HEADS, PEER_TT), lambda i, e: (0, i)),
        ],
        out_specs=pl.BlockSpec((PEER_TT, d), lambda i, e: (i, 0)),
        out_shape=jax.ShapeDtypeStruct((t, d), BF16),
        scratch_shapes=[pltpu.VMEM((d, PEER_TT), F32)],
        compiler_params=pltpu.CompilerParams(
            dimension_semantics=("parallel", "arbitrary"), vmem_limit_bytes=VMEM_LIMIT),
    )(x, u, vt, s1, e1, s2, e2, tau)


def _extract_top(work_ref, top_ref, n_groups):
    n_rows = work_ref.shape[1]
    iota = lax.broadcasted_iota(jnp.int32, (n_rows, LANES), 0)

    def body(r, carry):
        for g in range(n_groups):
            w = work_ref[g]
            m = jnp.max(w, axis=0, keepdims=True)
            top_ref[g, pl.ds(r, 1), :] = m
            first = jnp.min(jnp.where(w == m, iota, n_rows), axis=0, keepdims=True)
            work_ref[g] = jnp.where(iota == first, -jnp.inf, w)
        return carry

    lax.fori_loop(0, PEER_TOPK, body, 0)


def _peer_select_kernel(q_ref, keys_ref, s1_ref, e1_ref, s2_ref, e2_ref, tau_ref,
                        work_ref, top_ref, cand_ref, cwork_ref, ctop_ref):
    for h in range(PEER_HEADS):
        for p in range(2):
            g = 2 * h + p
            sc = lax.dot_general(keys_ref[h, p], q_ref[:, g * N_KEYS:(g + 1) * N_KEYS],
                                 (((1,), (1,)), ((), ())), preferred_element_type=F32,
                                 precision=lax.Precision.HIGHEST)
            work_ref[g] = sc
            (s1_ref, s2_ref)[p][h] = sc
    _extract_top(work_ref, top_ref, 2 * PEER_HEADS)
    for h in range(PEER_HEADS):
        t2 = top_ref[2 * h + 1]
        for p in range(PEER_TOPK):
            c = top_ref[2 * h, p:p + 1, :] + t2
            cand_ref[h, p * PEER_TOPK:(p + 1) * PEER_TOPK, :] = c
            cwork_ref[h, p * PEER_TOPK:(p + 1) * PEER_TOPK, :] = c
    _extract_top(cwork_ref, ctop_ref, PEER_HEADS)
    for h in range(PEER_HEADS):
        tau = ctop_ref[h, PEER_TOPK - 1:PEER_TOPK, :]
        m1 = top_ref[2 * h, 0:1, :]
        m2 = top_ref[2 * h + 1, 0:1, :]
        cand = cand_ref[h]
        z = jnp.sum(jnp.where(cand >= tau, jnp.exp(cand - (m1 + m2)), 0.0), axis=0, keepdims=True)
        tau_ref[h:h + 1, :] = tau
        e1_ref[h] = jnp.exp(s1_ref[h] - m1)
        e2_ref[h] = jnp.exp(s2_ref[h] - m2) / z


def peer_select(q, keys):
    t = q.shape[0]
    big = jax.ShapeDtypeStruct((PEER_HEADS, N_KEYS, t), F32)
    big_spec = pl.BlockSpec((PEER_HEADS, N_KEYS, LANES), lambda i: (0, 0, i))
    n_cand = PEER_TOPK * PEER_TOPK
    return pl.pallas_call(
        _peer_select_kernel,
        grid=(t // LANES,),
        in_specs=[pl.BlockSpec((LANES, q.shape[1]), lambda i: (i, 0)),
                  pl.BlockSpec(keys.shape, lambda i: (0, 0, 0, 0))],
        out_specs=[big_spec, big_spec, big_spec, big_spec,
                   pl.BlockSpec((PEER_HEADS, LANES), lambda i: (0, i))],
        out_shape=[big, big, big, big, jax.ShapeDtypeStruct((PEER_HEADS, t), F32)],
        scratch_shapes=[pltpu.VMEM((2 * PEER_HEADS, N_KEYS, LANES), F32),
                        pltpu.VMEM((2 * PEER_HEADS, PEER_TOPK, LANES), F32),
                        pltpu.VMEM((PEER_HEADS, n_cand, LANES), F32),
                        pltpu.VMEM((PEER_HEADS, n_cand, LANES), F32),
                        pltpu.VMEM((PEER_HEADS, PEER_TOPK, LANES), F32)],
        compiler_params=pltpu.CompilerParams(
            dimension_semantics=("parallel",), vmem_limit_bytes=VMEM_LIMIT),
    )(q, keys)


def peer_ffn(f_bf, wq, sub_keys, u_bf, vt_bf):
    q = matmul(f_bf, wq, F32)
    s1, e1, s2, e2, tau = peer_select(q, sub_keys)
    return peer_dense(f_bf, u_bf, vt_bf, s1, e1, s2, e2, tau)


SCAN_ROWS = 128


def _scan_block(step, n_blocks, n_lat_blocks, reverse):
    return (n_blocks - 1 - step) if reverse else (step + n_lat_blocks) % n_blocks


def _dot(a, b, dims=(((1,), (0,)), ((), ())), precision=None):
    if precision is None:
        a, b = a.astype(BF16), b.astype(BF16)
    return lax.dot_general(a, b, dims, preferred_element_type=F32, precision=precision)


_NT = (((1,), (1,)), ((), ()))
_TN = (((0,), (0,)), ((), ()))


RET_LOG_GAMMA = [math.log1p(-(2.0 ** (-5.0 - h))) for h in range(RET_HEADS)]


def _ret_consts(reverse):
    lg = np.array(RET_LOG_GAMMA[::-1] if reverse else RET_LOG_GAMMA, np.float64)[:, None, None]
    pos = np.arange(RET_CHUNK, dtype=np.float64)
    diff = pos[:, None] - pos[None, :]
    rank = pos
    if reverse:
        diff, rank = -diff, RET_CHUNK - 1.0 - pos
    ones = np.ones((1, 1, HEAD_DIM))
    intra = np.where(diff >= 0, np.exp(lg * np.maximum(diff, 0.0)), 0.0)
    qdec = np.exp(lg * (rank + 1.0)[None, :, None]) * ones
    kdec = np.exp(lg * (RET_CHUNK - 1.0 - rank)[None, :, None]) * ones
    cdec = np.exp(lg * RET_CHUNK) * ones
    return [jnp.asarray(a, F32) for a in (intra, qdec, kdec, cdec)]


def _ret_scan_kernel(q_ref, k_ref, v_ref, c_ref, s_ref, intra_ref, qdec_ref, kdec_ref, cdec_ref,
                     o_ref, state_ref):
    @pl.when(pl.program_id(1) == 0)
    def _():
        state_ref[...] = jnp.zeros_like(state_ref)

    c = c_ref[...]
    s = s_ref[...]
    for h in range(RET_HEADS):
        lane = slice(h * HEAD_DIM, (h + 1) * HEAD_DIM)
        q = q_ref[0, :, lane]
        k = k_ref[0, :, lane]
        v = v_ref[0, :, lane]
        q = q * c + pltpu.roll(q, HEAD_DIM // 2, 1) * s
        k = (k * c + pltpu.roll(k, HEAD_DIM // 2, 1) * s) * HEAD_DIM ** -0.5
        st = state_ref[h]
        scores = _dot(q, k, _NT) * intra_ref[h]
        o_ref[0, :, lane] = _dot(scores, v) + _dot(q * qdec_ref[h], st)
        state_ref[h] = st * cdec_ref[h] + _dot(k * kdec_ref[h], v, _TN)


def retention_scan(p_ret, tab_c, tab_s, n_lat, reverse):
    b, n, _ = p_ret.shape
    nb, nlb = n // RET_CHUNK, n_lat // RET_CHUNK
    consts = _ret_consts(reverse)

    def blk(s):
        return _scan_block(s, nb, nlb, reverse)

    def col(j):
        return pl.BlockSpec((1, RET_CHUNK, RET_W), lambda bi, s: (bi, blk(s), j))

    tab = pl.BlockSpec((RET_CHUNK, HEAD_DIM), lambda bi, s: (blk(s), 0))
    return pl.pallas_call(
        _ret_scan_kernel,
        grid=(b, nb),
        in_specs=[col(0), col(1), col(2), tab, tab]
        + [pl.BlockSpec(a.shape, lambda bi, s: (0, 0, 0)) for a in consts],
        out_specs=pl.BlockSpec((1, RET_CHUNK, RET_W), lambda bi, s: (bi, blk(s), 0)),
        out_shape=jax.ShapeDtypeStruct((b, n, RET_W), F32),
        scratch_shapes=[pltpu.VMEM((RET_HEADS, HEAD_DIM, HEAD_DIM), F32)],
        compiler_params=pltpu.CompilerParams(
            dimension_semantics=("parallel", "arbitrary"), vmem_limit_bytes=VMEM_LIMIT),
    )(p_ret, p_ret, p_ret, tab_c, tab_s, *consts)


def _silu(x):
    return x * jax.nn.sigmoid(x)


def _ret_final_kernel(of_ref, ob_ref, g_ref, w_ref, o_ref):
    for h in range(RET_HEADS):
        lane = slice(h * HEAD_DIM, (h + 1) * HEAD_DIM)
        o = of_ref[:, lane] + ob_ref[:, lane]
        mu = jnp.mean(o, axis=1, keepdims=True)
        d = o - mu
        var = jnp.mean(d * d, axis=1, keepdims=True)
        y = d * lax.rsqrt(var + NORM_EPS) * w_ref[:, lane] * _silu(g_ref[:, lane])
        o_ref[:, lane] = y.astype(o_ref.dtype)


def _gdn_final_kernel(of_ref, ob_ref, g_ref, w_ref, o_ref):
    for h in range(GDN_HEADS):
        lane = slice(h * HEAD_DIM, (h + 1) * HEAD_DIM)
        o = of_ref[:, lane] + ob_ref[:, lane]
        ms = jnp.mean(o * o, axis=1, keepdims=True)
        y = o * lax.rsqrt(ms + NORM_EPS) * w_ref[...] * _silu(g_ref[:, lane])
        o_ref[:, lane] = y.astype(o_ref.dtype)


FINAL_TT = 256


def mixer_finalize(kernel_fn, o_f, o_b, proj, gate_col, w):
    t, wd = o_f.shape
    row = pl.BlockSpec((FINAL_TT, wd), lambda i: (i, 0))
    return pl.pallas_call(
        kernel_fn,
        grid=(t // FINAL_TT,),
        in_specs=[row, row, pl.BlockSpec((FINAL_TT, wd), lambda i: (i, gate_col)),
                  pl.BlockSpec(w.shape, lambda i: (0, 0))],
        out_specs=row,
        out_shape=jax.ShapeDtypeStruct((t, wd), BF16),
        compiler_params=pltpu.CompilerParams(
            dimension_semantics=("parallel",), vmem_limit_bytes=VMEM_LIMIT),
    )(o_f, o_b, proj, w)


def ret_tables(cos, sin, n_ctx):
    c = jnp.concatenate([cos, cos], axis=1)
    s = jnp.concatenate([-sin, sin], axis=1)
    return (jnp.concatenate([c, jnp.ones((n_ctx, HEAD_DIM), F32)], axis=0),
            jnp.concatenate([s, jnp.zeros((n_ctx, HEAD_DIM), F32)], axis=0))


def retention_mixer(p_ret, norm_w, tabs, n_lat):
    b, n, _ = p_ret.shape
    o_f = retention_scan(p_ret, *tabs, n_lat, False)
    o_b = retention_scan(p_ret, *tabs, n_lat, True)
    out = mixer_finalize(_ret_final_kernel, o_f.reshape(b * n, RET_W), o_b.reshape(b * n, RET_W),
                         p_ret.reshape(b * n, 4 * RET_W), 3, norm_w[None, :])
    return out.reshape(b, n, RET_W)


GDN_TT = 256
CONV_HALO = SUBLANES


def _gdn_conv_kernel(main_ref, prev_ref, next_ref, w_ref, o_ref, ext_ref, *, tiles, lat_tiles):
    i = pl.program_id(0) % tiles
    sec = pl.program_id(1)
    seg_first = jnp.logical_or(i == 0, i == lat_tiles)
    seg_last = jnp.logical_or(i == lat_tiles - 1, i == tiles - 1)
    ext_ref[0:CONV_HALO, :] = jnp.where(seg_first, 0.0, prev_ref[...])
    ext_ref[CONV_HALO:CONV_HALO + GDN_TT, :] = main_ref[...]
    ext_ref[CONV_HALO + GDN_TT:, :] = jnp.where(seg_last, 0.0, next_ref[...])
    acc = jnp.zeros((GDN_TT, GDN_W), F32)
    for j in range(CONV_W):
        lo = CONV_HALO - CONV_W // 2 + j
        acc = acc + ext_ref[lo:lo + GDN_TT, :] * w_ref[j:j + 1, :]
    y = _silu(acc)
    scale = jnp.where(sec == 0, HEAD_DIM ** -0.5, 1.0)
    for h in range(GDN_HEADS):
        lane = slice(h * HEAD_DIM, (h + 1) * HEAD_DIM)
        yh = y[:, lane]
        nh = yh * (lax.rsqrt(jnp.sum(yh * yh, axis=1, keepdims=True) + NORM_EPS) * scale)
        o_ref[0, :, lane] = jnp.where(sec == 2, yh, nh)


def gdn_conv(p_gdn, conv_w, n_tot, n_lat):
    t = p_gdn.shape[0]
    tiles, lat_tiles = n_tot // GDN_TT, n_lat // GDN_TT
    per = GDN_TT // CONV_HALO
    last = t // CONV_HALO - 1
    return pl.pallas_call(
        functools.partial(_gdn_conv_kernel, tiles=tiles, lat_tiles=lat_tiles),
        grid=(t // GDN_TT, 3),
        in_specs=[pl.BlockSpec((GDN_TT, GDN_W), lambda i, s: (i, s)),
                  pl.BlockSpec((CONV_HALO, GDN_W), lambda i, s: (jnp.maximum(i * per - 1, 0), s)),
                  pl.BlockSpec((CONV_HALO, GDN_W), lambda i, s: (jnp.minimum((i + 1) * per, last), s)),
                  pl.BlockSpec((CONV_W, GDN_W), lambda i, s: (0, s))],
        out_specs=pl.BlockSpec((1, GDN_TT, GDN_W), lambda i, s: (s, i, 0)),
        out_shape=jax.ShapeDtypeStruct((3, t, GDN_W), F32),
        scratch_shapes=[pltpu.VMEM((GDN_TT + 2 * CONV_HALO, GDN_W), F32)],
        compiler_params=pltpu.CompilerParams(
            dimension_semantics=("parallel", "parallel"), vmem_limit_bytes=VMEM_LIMIT),
    )(p_gdn, p_gdn, p_gdn, conv_w)


def _gdn_gate_kernel(a_ref, alog_ref, dt_ref, sel_ref, gf_ref, gb_ref, bf_ref, bb_ref):
    x = a_ref[...]
    lane = lax.broadcasted_iota(jnp.int32, x.shape, 1)
    g = -jnp.exp(alog_ref[...]) * jax.nn.softplus(x + dt_ref[...])
    gate = jnp.where(lane < 2 * GDN_HEADS, g, jax.nn.sigmoid(x))
    for idx, ref in enumerate((gf_ref, gb_ref, bf_ref, bb_ref)):
        ref[...] = _dot(gate, sel_ref[idx], precision=lax.Precision.HIGHEST)


def gdn_gates(p_sml, a_log, dt_bias):
    t = p_sml.shape[0]
    pad = LANES - 2 * GDN_HEADS
    alog = jnp.pad(a_log.reshape(1, 2 * GDN_HEADS).astype(F32), ((0, 0), (0, pad)))
    dt = jnp.pad(dt_bias.reshape(1, 2 * GDN_HEADS).astype(F32), ((0, 0), (0, pad)))
    sel = np.zeros((4, LANES, GDN_W), np.float32)
    for idx in range(4):
        for h in range(GDN_HEADS):
            sel[idx, idx * GDN_HEADS + h, h * HEAD_DIM:(h + 1) * HEAD_DIM] = 1.0
    wide = jax.ShapeDtypeStruct((t, GDN_W), F32)
    wide_spec = pl.BlockSpec((GDN_TT, GDN_W), lambda i: (i, 0))
    vec_spec = pl.BlockSpec((1, LANES), lambda i: (0, 0))
    return pl.pallas_call(
        _gdn_gate_kernel,
        grid=(t // GDN_TT,),
        in_specs=[pl.BlockSpec((GDN_TT, LANES), lambda i: (i, 0)), vec_spec, vec_spec,
                  pl.BlockSpec(sel.shape, lambda i: (0, 0, 0))],
        out_specs=[wide_spec] * 4,
        out_shape=[wide] * 4,
        compiler_params=pltpu.CompilerParams(
            dimension_semantics=("parallel",), vmem_limit_bytes=VMEM_LIMIT),
    )(p_sml, alog, dt, jnp.asarray(sel))


def _gdn_scan_kernel(q_ref, k_ref, v_ref, g_ref, b_ref, o_ref, state_ref, *, reverse):
    @pl.when(pl.program_id(1) == 0)
    def _():
        state_ref[...] = jnp.zeros_like(state_ref)

    cl = GDN_CHUNK
    hi = lax.Precision.HIGHEST
    r = lax.broadcasted_iota(jnp.int32, (cl, cl), 0)
    c = lax.broadcasted_iota(jnp.int32, (cl, cl), 1)
    inc = (r <= c) if reverse else (r >= c)
    inc_f = inc.astype(F32)
    inc_t = ((c <= r) if reverse else (c >= r)).astype(F32)
    eye = (r == c).astype(F32)
    strict_f = inc_f - eye
    ones = jnp.ones((cl, HEAD_DIM), F32)
    n_chunks = SCAN_ROWS // cl
    order = range(n_chunks - 1, -1, -1) if reverse else range(n_chunks)
    heads = range(GDN_HEADS)
    items = [(h, ci) for ci in order for h in heads]

    def load(ref, lead):
        return [ref[lead + (slice(ci * cl, (ci + 1) * cl), slice(h * HEAD_DIM, (h + 1) * HEAD_DIM))]
                for h, ci in items]

    q, k, v = load(q_ref, (0, 0)), load(k_ref, (0, 0)), load(v_ref, (0, 0))
    gc, bc = load(g_ref, (0,)), load(b_ref, (0,))
    n = range(len(items))
    gcum = [_dot(inc_f, gc[i], precision=hi) for i in n]
    g_rows = [_dot(ones, gc[i], _NT, precision=hi) * (1.0 / HEAD_DIM) for i in n]
    gcum_row = [_dot(g_rows[i], inc_t, precision=hi) for i in n]
    gtot = [jnp.sum(gc[i], axis=0, keepdims=True) for i in n]
    decay = [jnp.where(inc, jnp.exp(jnp.where(inc, gcum[i][:, :cl] - gcum_row[i], 0.0)), 0.0) for i in n]
    eg = [jnp.exp(gcum[i]) for i in n]
    kb = [k[i] * bc[i] for i in n]
    a_pow = [_dot(kb[i], k[i], _NT) * decay[i] * strict_f for i in n]
    t_mat = [eye - a_pow[i] for i in n]
    for _ in range(int(math.log2(cl)) - 1):
        a_pow = [_dot(a_pow[i], a_pow[i]) for i in n]
        t_mat = [t_mat[i] + _dot(t_mat[i], a_pow[i]) for i in n]
    u = [_dot(t_mat[i], v[i] * bc[i]) for i in n]
    w = [_dot(t_mat[i], kb[i] * eg[i]) for i in n]
    attn = [_dot(q[i], k[i], _NT) * decay[i] for i in n]
    qd = [q[i] * eg[i] for i in n]
    kd = [k[i] * jnp.exp(gtot[i] - gcum[i]) for i in n]
    cd = [jnp.exp(gtot[i]) for i in n]
    st = [state_ref[h] for h in heads]
    for pos, ci in enumerate(order):
        idx = [pos * GDN_HEADS + h for h in heads]
        v_new = [u[i] - _dot(w[i], st[h]) for h, i in zip(heads, idx)]
        out = [_dot(qd[i], st[h]) + _dot(attn[i], v_new[h]) for h, i in zip(heads, idx)]
        st = [st[h] * cd[i] + _dot(kd[i], v_new[h], _TN) for h, i in zip(heads, idx)]
        for h in heads:
            o_ref[0, ci * cl:(ci + 1) * cl, h * HEAD_DIM:(h + 1) * HEAD_DIM] = out[h]
    for h in heads:
        state_ref[h] = st[h]


def gdn_scan(qkv, g, beta, n_lat, reverse):
    _, b, n, _ = qkv.shape
    nb, nlb = n // SCAN_ROWS, n_lat // SCAN_ROWS

    def blk(s):
        return _scan_block(s, nb, nlb, reverse)

    def sec(j):
        return pl.BlockSpec((1, 1, SCAN_ROWS, GDN_W), lambda bi, s: (j, bi, blk(s), 0))

    row = pl.BlockSpec((1, SCAN_ROWS, GDN_W), lambda bi, s: (bi, blk(s), 0))
    return pl.pallas_call(
        functools.partial(_gdn_scan_kernel, reverse=reverse),
        grid=(b, nb),
        in_specs=[sec(0), sec(1), sec(2), row, row],
        out_specs=row,
        out_shape=jax.ShapeDtypeStruct((b, n, GDN_W), F32),
        scratch_shapes=[pltpu.VMEM((GDN_HEADS, HEAD_DIM, HEAD_DIM), F32)],
        compiler_params=pltpu.CompilerParams(
            dimension_semantics=("parallel", "arbitrary"), vmem_limit_bytes=VMEM_LIMIT),
    )(qkv, qkv, qkv, g, beta)


def gdn_mixer(p_gdn, p_sml, lp, n_lat):
    b, n, _ = p_gdn.shape
    p_gdn2 = p_gdn.reshape(b * n, 4 * GDN_W)
    qkv = gdn_conv(p_gdn2, lp['conv_w'], n, n_lat).reshape(3, b, n, GDN_W)
    g_f, g_b, be_f, be_b = (a.reshape(b, n, GDN_W) for a in
                            gdn_gates(p_sml.reshape(b * n, -1), lp['gdn_a_log'], lp['gdn_dt_bias']))
    o_f = gdn_scan(qkv, g_f, be_f, n_lat, False)
    o_b = gdn_scan(qkv, g_b, be_b, n_lat, True)
    out = mixer_finalize(_gdn_final_kernel, o_f.reshape(b * n, GDN_W), o_b.reshape(b * n, GDN_W),
                         p_gdn2, 3, lp['gdn_norm_w'][None, :])
    return out.reshape(b, n, GDN_W)


def rms_norm(x, w):
    xf = x.astype(F32)
    y = xf * lax.rsqrt(jnp.mean(xf * xf, axis=-1, keepdims=True) + NORM_EPS)
    return (y * w.astype(F32)).astype(x.dtype)


def modulate(h, shift, scale):
    return h * (1 + scale) + shift


def split_cols(t, sizes):
    parts, start = [], 0
    for s in sizes:
        parts.append(t[..., start:start + s])
        start += s
    return parts


def axial_rope_tables(row, col, rot_dim):
    n_freq = rot_dim // 4
    inv_freq = ROPE_BASE ** (-jnp.arange(n_freq, dtype=F32) / n_freq)
    ang = jnp.concatenate([row[:, None] * inv_freq, col[:, None] * inv_freq], axis=-1)
    return jnp.cos(ang), jnp.sin(ang)


MLA_HP = 2 * LANES
MLA_TT = 256
MLA_TQ = 512
MLA_TK = 512


def _rope_tail(tail, c, s1, s2):
    half = ROPE_DIM // 2
    return tail * c + pltpu.roll(tail, LANES - half, 1) * s1 + pltpu.roll(tail, half, 1) * s2


def _mla_q_prep_kernel(x_ref, w_ref, c_ref, s1_ref, s2_ref, o_ref):
    x = x_ref[...]
    ms = jnp.sum(x * x, axis=1, keepdims=True) * (1.0 / MLA_QK)
    y = x * lax.rsqrt(ms + NORM_EPS) * w_ref[...]
    tail = _rope_tail(y[:, NOPE_DIM:], c_ref[...], s1_ref[...], s2_ref[...])
    o_ref[:, :NOPE_DIM] = (y[:, :NOPE_DIM] * MLA_QK ** -0.5).astype(o_ref.dtype)
    o_ref[:, NOPE_DIM:] = (tail * MLA_QK ** -0.5).astype(o_ref.dtype)


def _mla_kv_prep_kernel(kv_ref, kr_ref, w_ref, c_ref, s1_ref, s2_ref, k_ref, v_ref):
    nope = kv_ref[:, :NOPE_DIM]
    kr = kr_ref[...]
    ms = (jnp.sum(nope * nope, axis=1, keepdims=True)
          + jnp.sum(kr * kr, axis=1, keepdims=True)) * (1.0 / MLA_QK)
    r = lax.rsqrt(ms + NORM_EPS)
    k_ref[:, :NOPE_DIM] = (nope * r * w_ref[:, :NOPE_DIM]).astype(k_ref.dtype)
    tail = _rope_tail(kr * r * w_ref[:, NOPE_DIM:], c_ref[...], s1_ref[...], s2_ref[...])
    k_ref[:, NOPE_DIM:] = tail.astype(k_ref.dtype)
    v_ref[...] = kv_ref[:, NOPE_DIM:].astype(v_ref.dtype)


def mla_prep(qraw, kvraw, kr_pad, qw_pad, kw_pad, tabs, n_tot):
    t = qraw.shape[0]
    tiles = n_tot // MLA_TT
    tab_spec = pl.BlockSpec((MLA_TT, LANES), lambda i, h: (i % tiles, 0))
    w_spec = pl.BlockSpec((1, MLA_HP), lambda i, h: (0, 0))
    head_spec = pl.BlockSpec((MLA_TT, MLA_HP), lambda i, h: (i, h))
    params = pltpu.CompilerParams(dimension_semantics=("parallel", "parallel"), vmem_limit_bytes=VMEM_LIMIT)
    q = pl.pallas_call(
        _mla_q_prep_kernel,
        grid=(t // MLA_TT, MLA_HEADS),
        in_specs=[head_spec, w_spec, tab_spec, tab_spec, tab_spec],
        out_specs=head_spec,
        out_shape=jax.ShapeDtypeStruct((t, MLA_HEADS * MLA_HP), BF16),
        compiler_params=params,
    )(qraw, qw_pad, *tabs)
    k, v = pl.pallas_call(
        _mla_kv_prep_kernel,
        grid=(t // MLA_TT, MLA_HEADS),
        in_specs=[head_spec, pl.BlockSpec((MLA_TT, LANES), lambda i, h: (i, 0)), w_spec,
                  tab_spec, tab_spec, tab_spec],
        out_specs=[head_spec, pl.BlockSpec((MLA_TT, V_DIM), lambda i, h: (i, h))],
        out_shape=[jax.ShapeDtypeStruct((t, MLA_HEADS * MLA_HP), BF16),
                   jax.ShapeDtypeStruct((t, MLA_HEADS * V_DIM), BF16)],
        compiler_params=params,
    )(kvraw, kr_pad, kw_pad, *tabs)
    return q, k, v


def _mla_flash_kernel(q_ref, k_ref, v_ref, o_ref, *, n_loop, tail_len):
    q = q_ref[0]
    tq = q.shape[0]

    def attend(kc, vc, carry):
        m, l, acc = carry
        s = lax.dot_general(q, kc, (((1,), (1,)), ((), ())), preferred_element_type=F32)
        m_new = jnp.maximum(m, jnp.max(s, axis=1, keepdims=True))
        alpha = jnp.exp(m - m_new)
        p = jnp.exp(s - m_new)
        l = alpha * l + jnp.sum(p, axis=1, keepdims=True)
        acc = alpha * acc + jnp.dot(p.astype(vc.dtype), vc, preferred_element_type=F32)
        return m_new, l, acc

    def body(c, carry):
        off = pl.multiple_of(c * MLA_TK, MLA_TK)
        return attend(k_ref[0, pl.ds(off, MLA_TK), :], v_ref[0, pl.ds(off, MLA_TK), :], carry)

    carry = (jnp.full((tq, 1), -jnp.inf, F32), jnp.zeros((tq, 1), F32), jnp.zeros((tq, V_DIM), F32))
    if n_loop:
        carry = lax.fori_loop(0, n_loop, body, carry)
    if tail_len:
        lo = n_loop * MLA_TK
        carry = attend(k_ref[0, lo:lo + tail_len, :], v_ref[0, lo:lo + tail_len, :], carry)
    _, l, acc = carry
    o_ref[0] = (acc / l).astype(o_ref.dtype)


def mla_attend(q, k, v, n_lat, n_ctx, context_queries):
    b = q.shape[0]
    params = pltpu.CompilerParams(dimension_semantics=("parallel", "parallel", "arbitrary"),
                                  vmem_limit_bytes=VMEM_LIMIT)
    if context_queries:
        blk = n_lat // n_ctx
        return pl.pallas_call(
            functools.partial(_mla_flash_kernel, n_loop=0, tail_len=n_ctx),
            grid=(b, MLA_HEADS, 1),
            in_specs=[pl.BlockSpec((1, n_ctx, MLA_HP), lambda bi, h, i: (bi, blk, h)),
                      pl.BlockSpec((1, n_ctx, MLA_HP), lambda bi, h, i: (bi, blk, h)),
                      pl.BlockSpec((1, n_ctx, V_DIM), lambda bi, h, i: (bi, blk, h))],
            out_specs=pl.BlockSpec((1, n_ctx, V_DIM), lambda bi, h, i: (bi, 0, h)),
            out_shape=jax.ShapeDtypeStruct((b, n_ctx, MLA_HEADS * V_DIM), BF16),
            compiler_params=params,
        )(q, k, v)
    n_tot = n_lat + n_ctx
    return pl.pallas_call(
        functools.partial(_mla_flash_kernel, n_loop=n_lat // MLA_TK, tail_len=n_ctx),
        grid=(b, MLA_HEADS, n_lat // MLA_TQ),
        in_specs=[pl.BlockSpec((1, MLA_TQ, MLA_HP), lambda bi, h, i: (bi, i, h)),
                  pl.BlockSpec((1, n_tot, MLA_HP), lambda bi, h, i: (bi, 0, h)),
                  pl.BlockSpec((1, n_tot, V_DIM), lambda bi, h, i: (bi, 0, h))],
        out_specs=pl.BlockSpec((1, MLA_TQ, V_DIM), lambda bi, h, i: (bi, i, h)),
        out_shape=jax.ShapeDtypeStruct((b, n_lat, MLA_HEADS * V_DIM), BF16),
        compiler_params=params,
    )(q, k, v)


def mla_tables(cos, sin, n_ctx):
    n_lat, half = cos.shape
    z = jnp.zeros((n_lat, LANES - 2 * half), F32)
    zh = jnp.zeros((n_lat, half), F32)
    c = jnp.concatenate([cos, cos, z], axis=1)
    s1 = jnp.concatenate([-sin, zh, z], axis=1)
    s2 = jnp.concatenate([zh, sin, z], axis=1)
    ident = jnp.concatenate([jnp.ones((n_ctx, 2 * half), F32), jnp.zeros((n_ctx, LANES - 2 * half), F32)], axis=1)
    zero = jnp.zeros((n_ctx, LANES), F32)
    return (jnp.concatenate([c, ident], axis=0), jnp.concatenate([s1, zero], axis=0),
            jnp.concatenate([s2, zero], axis=0))


def mla_mixer(c_q, c_kv, k_rope, lp, tabs, n_lat, with_ctx):
    b, n, _ = c_q.shape
    n_ctx = n - n_lat
    qraw = matmul(rms_norm(c_q, lp['cq_norm_w']).reshape(b * n, Q_LORA).astype(BF16), lp['w_uq_pad'])
    kvraw = matmul(rms_norm(c_kv, lp['ckv_norm_w']).reshape(b * n, KV_LORA).astype(BF16), lp['w_ukv'])
    kr_pad = jnp.pad(k_rope.reshape(b * n, ROPE_DIM), ((0, 0), (0, LANES - ROPE_DIM)))
    q, k, v = mla_prep(qraw, kvraw, kr_pad, lp['q_norm_pad'], lp['k_norm_pad'], tabs, n)
    q = q.reshape(b, n, -1)
    k = k.reshape(b, n, -1)
    v = v.reshape(b, n, -1)
    o_x = mla_attend(q, k, v, n_lat, n_ctx, False)
    if not with_ctx:
        return o_x
    return jnp.concatenate([o_x, mla_attend(q, k, v, n_lat, n_ctx, True)], axis=1)


def trunk_layer(x, ctx, mod_x, mod_c, lp, ret_tabs, mla_tabs, update_ctx):
    b, n_ctx, d = ctx.shape
    n_lat = x.shape[1]
    n_tot = n_ctx + n_lat
    sh1, sc1, g1, sh2, sc2, g2 = (m[:, None, :] for m in jnp.split(mod_x, 6, axis=-1))
    csh1, csc1, cg1, csh2, csc2, cg2 = jnp.split(mod_c, 6, axis=-1)

    h = jnp.concatenate([modulate(rms_norm(x, lp['norm1_w']), sh1, sc1),
                         modulate(rms_norm(ctx, lp['norm1_w']), csh1, csc1)], axis=1)
    h2 = h.reshape(b * n_tot, d).astype(BF16)
    p_ret = matmul(h2, lp['w_in_ret']).reshape(b, n_tot, 4 * RET_W)
    p_gdn = matmul(h2, lp['w_in_gdn']).reshape(b, n_tot, 4 * GDN_W)
    p_sml = matmul(h2, lp['w_in_sml'], tn=2048).reshape(b, n_tot, -1)
    _, _, _, _, c_q, c_kv, k_rope = split_cols(p_sml, (GDN_HEADS,) * 4 + (Q_LORA, KV_LORA, ROPE_DIM))
    ret_o = retention_mixer(p_ret, lp['ret_norm_w'], ret_tabs, n_lat)
    gdn_o = gdn_mixer(p_gdn, p_sml, lp, n_lat)
    mla_o = mla_mixer(c_q, c_kv, k_rope, lp, mla_tabs, n_lat, update_ctx)
    if update_ctx:
        mixed = jnp.concatenate([ret_o, gdn_o, mla_o], axis=-1)
        y = matmul(mixed.reshape(b * n_tot, MIX_W), lp['w_out']).reshape(b, n_tot, d)
        x = x + g1 * y[:, :n_lat]
        ctx = ctx + cg1 * y[:, n_lat:]
        f = jnp.concatenate([modulate(rms_norm(x, lp['norm2_w']), sh2, sc2),
                             modulate(rms_norm(ctx, lp['norm2_w']), csh2, csc2)], axis=1)
        f = peer_ffn(f.reshape(b * n_tot, d).astype(BF16), lp['peer_wq'], lp['peer_keys'], lp['peer_u'],
                     lp['peer_vt']).reshape(b, n_tot, d)
        x = x + g2 * f[:, :n_lat]
        ctx = ctx + cg2 * f[:, n_lat:]
    else:
        mixed = jnp.concatenate([ret_o[:, :n_lat], gdn_o[:, :n_lat], mla_o], axis=-1)
        y = matmul(mixed.reshape(b * n_lat, MIX_W), lp['w_out']).reshape(b, n_lat, d)
        x = x + g1 * y
        f = modulate(rms_norm(x, lp['norm2_w']), sh2, sc2)
        f = peer_ffn(f.reshape(b * n_lat, d).astype(BF16), lp['peer_wq'], lp['peer_keys'], lp['peer_u'],
                     lp['peer_vt'])
        x = x + g2 * f.reshape(b, n_lat, d)
    return x, ctx


def kernel(x, c, ctx, c_ctx, ada_w, ada_b, norm1_w, norm2_w, w_in, conv_w, gdn_a_log, gdn_dt_bias,
           gdn_norm_w, ret_norm_w, cq_norm_w, ckv_norm_w, w_uq, w_ukv, q_norm_w, k_norm_w, w_out,
           peer_wq, peer_keys, peer_u, peer_v):
    n_lat = x.shape[1]
    n_rows = n_lat // GRID_W
    row = jnp.repeat(jnp.arange(n_rows, dtype=F32), GRID_W, total_repeat_length=n_lat)
    col = (jnp.arange(n_lat) % GRID_W).astype(F32)
    ret_tabs = ret_tables(*axial_rope_tables(row, col, HEAD_DIM), ctx.shape[1])
    mla_tabs = mla_tables(*axial_rope_tables(row, col, ROPE_DIM), ctx.shape[1])
    silu_c = jax.nn.silu(c)
    silu_cc = jax.nn.silu(c_ctx)
    sml_w = 4 * GDN_HEADS + Q_LORA + KV_LORA + ROPE_DIM
    sml_pad = -sml_w % LANES
    for i in range(DEPTH):
        mod_x = silu_c @ ada_w[i] + ada_b[i]
        mod_c = silu_cc @ ada_w[i] + ada_b[i]
        w_in_i = w_in[i].astype(BF16)
        lp = {
            'norm1_w': norm1_w[i], 'norm2_w': norm2_w[i], 'conv_w': conv_w[i],
            'w_in_ret': w_in_i[:, :4 * RET_W],
            'w_in_gdn': w_in_i[:, 4 * RET_W:4 * RET_W + 4 * GDN_W],
            'w_in_sml': jnp.pad(w_in_i[:, 4 * RET_W + 4 * GDN_W:], ((0, 0), (0, sml_pad))),
            'gdn_a_log': gdn_a_log[i], 'gdn_dt_bias': gdn_dt_bias[i], 'gdn_norm_w': gdn_norm_w[i],
            'ret_norm_w': ret_norm_w[i], 'cq_norm_w': cq_norm_w[i], 'ckv_norm_w': ckv_norm_w[i],
            'w_uq_pad': jnp.pad(w_uq[i].astype(BF16).reshape(Q_LORA, MLA_HEADS, MLA_QK),
                                ((0, 0), (0, 0), (0, MLA_HP - MLA_QK))).reshape(Q_LORA, MLA_HEADS * MLA_HP),
            'w_ukv': w_ukv[i].astype(BF16),
            'q_norm_pad': jnp.pad(q_norm_w[i], (0, MLA_HP - MLA_QK))[None, :],
            'k_norm_pad': jnp.pad(k_norm_w[i], (0, MLA_HP - MLA_QK))[None, :],
            'w_out': w_out[i].astype(BF16), 'peer_wq': peer_wq[i].astype(BF16), 'peer_keys': peer_keys[i],
            'peer_u': peer_u[i].astype(BF16), 'peer_vt': peer_v[i].astype(BF16).T,
        }
        x, ctx = trunk_layer(x, ctx, mod_x, mod_c, lp, ret_tabs, mla_tabs, i < DEPTH - 1)
    return x
```

```python
import functools
import math

import jax
import jax.numpy as jnp
import numpy as np
from jax import lax
from jax.experimental import pallas as pl
from jax.experimental.pallas import tpu as pltpu

D_MODEL = 4096
DEPTH = 2
GRID_W = 64
HEAD_DIM = 128
RET_HEADS = 8
RET_CHUNK = 128
GDN_HEADS = 8
GDN_CHUNK = 64
CONV_W = 5
MLA_HEADS = 16
Q_LORA = 1024
KV_LORA = 512
NOPE_DIM = 128
ROPE_DIM = 64
V_DIM = 128
MLA_QK = NOPE_DIM + ROPE_DIM
ATTN_BLOCK = 128
PEER_HEADS = 8
N_KEYS = 128
N_EXPERTS = N_KEYS * N_KEYS
PEER_TOPK = 16
PEER_QDIM = 256
ROPE_BASE = 10000.0
NORM_EPS = 1e-6
RET_W = RET_HEADS * HEAD_DIM
GDN_W = GDN_HEADS * HEAD_DIM
MLA_W = MLA_HEADS * V_DIM
MIX_W = RET_W + GDN_W + MLA_W
IN_SIZES = (RET_W,) * 4 + (GDN_W,) * 4 + (GDN_HEADS,) * 4 + (Q_LORA, KV_LORA, ROPE_DIM)

LANES = 128
SUBLANES = 8
VMEM_LIMIT = 56 * 1024 * 1024

F32 = jnp.float32
BF16 = jnp.bfloat16


def _mm_kernel(a_ref, b_ref, o_ref):
    o_ref[...] = jnp.dot(a_ref[...].astype(BF16), b_ref[...].astype(BF16),
                         preferred_element_type=F32).astype(o_ref.dtype)


def _pick_tile(n, target, align):
    best = n
    for t in range(align, min(n, target) + 1, align):
        if n % t == 0:
            best = t
    return best


def matmul(a, b, out_dtype=F32, tm=512, tn=512):
    m, k = a.shape
    n = b.shape[1]
    tm = _pick_tile(m, tm, 8)
    tn = _pick_tile(n, tn, LANES)
    return pl.pallas_call(
        _mm_kernel,
        grid=(m // tm, n // tn),
        in_specs=[pl.BlockSpec((tm, k), lambda i, j: (i, 0)),
                  pl.BlockSpec((k, tn), lambda i, j: (0, j))],
        out_specs=pl.BlockSpec((tm, tn), lambda i, j: (i, j)),
        out_shape=jax.ShapeDtypeStruct((m, n), out_dtype),
        compiler_params=pltpu.CompilerParams(
            dimension_semantics=("parallel", "parallel"), vmem_limit_bytes=VMEM_LIMIT),
    )(a, b)


PEER_TT = 512
PEER_EB = 512
PEER_JB = PEER_EB // N_KEYS
assert SUBLANES == 2 * PEER_JB


def _gelu(a):
    return 0.5 * a * (1.0 + lax.erf(a * (2.0 ** -0.5)))


def _peer_kernel(x_ref, u_ref, vt_ref, s1_ref, e1_ref, s2_ref, e2_ref, tau_ref, o_ref, acc_ref):
    e = pl.program_id(1)

    @pl.when(e == 0)
    def _():
        acc_ref[...] = jnp.zeros_like(acc_ref)

    act_t = lax.dot_general(u_ref[...], x_ref[...], (((1,), (1,)), ((), ())),
                            preferred_element_type=F32)
    upper = (e % 2) == 1

    def key_rows(ref, h, lane):
        blk = ref[h, :, lane]
        return jnp.where(upper, pltpu.roll(blk, PEER_JB, 0), blk)

    rows = []
    for jj in range(PEER_JB):
        cols = []
        for ts in range(PEER_TT // LANES):
            lane = slice(ts * LANES, (ts + 1) * LANES)
            g = jnp.zeros((N_KEYS, LANES), F32)
            for h in range(PEER_HEADS):
                s1row = key_rows(s1_ref, h, lane)[jj:jj + 1, :]
                e1row = key_rows(e1_ref, h, lane)[jj:jj + 1, :]
                t = s2_ref[h, :, lane] + s1row
                g = g + jnp.where(t >= tau_ref[h:h + 1, lane], e2_ref[h, :, lane] * e1row, 0.0)
            a = act_t[jj * N_KEYS:(jj + 1) * N_KEYS, lane]
            cols.append((_gelu(a) * g).astype(BF16))
        rows.append(jnp.concatenate(cols, axis=1))
    c_t = jnp.concatenate(rows, axis=0)
    acc_ref[...] += jnp.dot(vt_ref[...], c_t, preferred_element_type=F32)

    @pl.when(e == pl.num_programs(1) - 1)
    def _():
        o_ref[...] = acc_ref[...].T.astype(o_ref.dtype)


def peer_dense(x, u, vt, s1, e1, s2, e2, tau):
    t, d = x.shape
    n_e = u.shape[0] // PEER_EB
    row_spec = pl.BlockSpec((PEER_HEADS, SUBLANES, PEER_TT),
                            lambda i, e: (0, e // (SUBLANES // PEER_JB), i))
    return pl.pallas_call(
        _peer_kernel,
        grid=(t // PEER_TT, n_e),
        in_specs=[
            pl.BlockSpec((PEER_TT, d), lambda i, e: (i, 0)),
            pl.BlockSpec((PEER_EB, d), lambda i, e: (e, 0)),
            pl.BlockSpec((d, PEER_EB), lambda i, e: (0, e)),
            row_spec,
            row_spec,
            pl.BlockSpec((PEER_HEADS, N_KEYS, PEER_TT), lambda i, e: (0, 0, i)),
            pl.BlockSpec((PEER_HEADS, N_KEYS, PEER_TT), lambda i, e: (0, 0, i)),
            pl.BlockSpec((PEER_HEADS, PEER_TT), lambda i, e: (0, i)),
        ],
        out_specs=pl.BlockSpec((PEER_TT, d), lambda i, e: (i, 0)),
        out_shape=jax.ShapeDtypeStruct((t, d), BF16),
        scratch_shapes=[pltpu.VMEM((d, PEER_TT), F32)],
        compiler_params=pltpu.CompilerParams(
            dimension_semantics=("parallel", "arbitrary"), vmem_limit_bytes=VMEM_LIMIT),
    )(x, u, vt, s1, e1, s2, e2, tau)


def _extract_top(work_ref, top_ref, n_groups):
    n_rows = work_ref.shape[1]
    iota = lax.broadcasted_iota(jnp.int32, (n_rows, LANES), 0)

    def body(r, carry):
        for g in range(n_groups):
            w = work_ref[g]
            m = jnp.max(w, axis=0, keepdims=True)
            top_ref[g, pl.ds(r, 1), :] = m
            first = jnp.min(jnp.where(w == m, iota, n_rows), axis=0, keepdims=True)
            work_ref[g] = jnp.where(iota == first, -jnp.inf, w)
        return carry

    lax.fori_loop(0, PEER_TOPK, body, 0)


def _peer_select_kernel(q_ref, keys_ref, s1_ref, e1_ref, s2_ref, e2_ref, tau_ref,
                        work_ref, top_ref, cand_ref, cwork_ref, ctop_ref):
    for h in range(PEER_HEADS):
        for p in range(2):
            g = 2 * h + p
            sc = lax.dot_general(keys_ref[h, p], q_ref[:, g * N_KEYS:(g + 1) * N_KEYS],
                                 (((1,), (1,)), ((), ())), preferred_element_type=F32,
                                 precision=lax.Precision.HIGHEST)
            work_ref[g] = sc
            (s1_ref, s2_ref)[p][h] = sc
    _extract_top(work_ref, top_ref, 2 * PEER_HEADS)
    half = PEER_TOPK // 2
    for h in range(PEER_HEADS):
        t1 = top_ref[2 * h]
        t2 = top_ref[2 * h + 1]
        pieces = [t1[0:1, :] + t2]
        pieces += [t1[p:p + 1, :] + t2[:half, :] for p in range(1, half)]
        pieces += [t1[half:, :] + t2[0:1, :]]
        lo = 0
        for c in pieces:
            cand_ref[h, lo:lo + c.shape[0], :] = c
            cwork_ref[h, lo:lo + c.shape[0], :] = c
            lo += c.shape[0]
    _extract_top(cwork_ref, ctop_ref, PEER_HEADS)
    for h in range(PEER_HEADS):
        tau = ctop_ref[h, PEER_TOPK - 1:PEER_TOPK, :]
        m1 = top_ref[2 * h, 0:1, :]
        m2 = top_ref[2 * h + 1, 0:1, :]
        cand = cand_ref[h]
        z = jnp.sum(jnp.where(cand >= tau, jnp.exp(cand - (m1 + m2)), 0.0), axis=0, keepdims=True)
        tau_ref[h:h + 1, :] = tau
        e1_ref[h] = jnp.exp(s1_ref[h] - m1)
        e2_ref[h] = jnp.exp(s2_ref[h] - m2) / z


def peer_select(q, keys):
    t = q.shape[0]
    big = jax.ShapeDtypeStruct((PEER_HEADS, N_KEYS, t), F32)
    big_spec = pl.BlockSpec((PEER_HEADS, N_KEYS, LANES), lambda i: (0, 0, i))
    n_cand = PEER_TOPK + (PEER_TOPK // 2 - 1) * (PEER_TOPK // 2) + PEER_TOPK // 2
    return pl.pallas_call(
        _peer_select_kernel,
        grid=(t // LANES,),
        in_specs=[pl.BlockSpec((LANES, q.shape[1]), lambda i: (i, 0)),
                  pl.BlockSpec(keys.shape, lambda i: (0, 0, 0, 0))],
        out_specs=[big_spec, big_spec, big_spec, big_spec,
                   pl.BlockSpec((PEER_HEADS, LANES), lambda i: (0, i))],
        out_shape=[big, big, big, big, jax.ShapeDtypeStruct((PEER_HEADS, t), F32)],
        scratch_shapes=[pltpu.VMEM((2 * PEER_HEADS, N_KEYS, LANES), F32),
                        pltpu.VMEM((2 * PEER_HEADS, PEER_TOPK, LANES), F32),
                        pltpu.VMEM((PEER_HEADS, n_cand, LANES), F32),
                        pltpu.VMEM((PEER_HEADS, n_cand, LANES), F32),
                        pltpu.VMEM((PEER_HEADS, PEER_TOPK, LANES), F32)],
        compiler_params=pltpu.CompilerParams(
            dimension_semantics=("parallel",), vmem_limit_bytes=VMEM_LIMIT),
    )(q, keys)


def peer_ffn(f_bf, wq, sub_keys, u_bf, vt_bf):
    q = matmul(f_bf, wq, F32)
    s1, e1, s2, e2, tau = peer_select(q, sub_keys)
    return peer_dense(f_bf, u_bf, vt_bf, s1, e1, s2, e2, tau)


SCAN_ROWS = 128


def _scan_block(step, n_blocks, n_lat_blocks, reverse):
    return (n_blocks - 1 - step) if reverse else (step + n_lat_blocks) % n_blocks


def _dot(a, b, dims=(((1,), (0,)), ((), ())), precision=None):
    if precision is None:
        a, b = a.astype(BF16), b.astype(BF16)
    return lax.dot_general(a, b, dims, preferred_element_type=F32, precision=precision)


_NT = (((1,), (1,)), ((), ()))
_TN = (((0,), (0,)), ((), ()))


RET_LOG_GAMMA = [math.log1p(-(2.0 ** (-5.0 - h))) for h in range(RET_HEADS)]


def _ret_consts(reverse):
    lg = np.array(RET_LOG_GAMMA[::-1] if reverse else RET_LOG_GAMMA, np.float64)[:, None, None]
    pos = np.arange(RET_CHUNK, dtype=np.float64)
    diff = pos[:, None] - pos[None, :]
    rank = pos
    if reverse:
        diff, rank = -diff, RET_CHUNK - 1.0 - pos
    ones = np.ones((1, 1, HEAD_DIM))
    intra = np.where(diff >= 0, np.exp(lg * np.maximum(diff, 0.0)), 0.0)
    qdec = np.exp(lg * (rank + 1.0)[None, :, None]) * ones
    kdec = np.exp(lg * (RET_CHUNK - 1.0 - rank)[None, :, None]) * ones
    cdec = np.exp(lg * RET_CHUNK) * ones
    return [jnp.asarray(a, F32) for a in (intra, qdec, kdec, cdec)]


def _ret_scan_kernel(q_ref, k_ref, v_ref, c_ref, s_ref, intra_ref, qdec_ref, kdec_ref, cdec_ref,
                     o_ref, state_ref):
    @pl.when(pl.program_id(1) == 0)
    def _():
        state_ref[...] = jnp.zeros_like(state_ref)

    c = c_ref[...]
    s = s_ref[...]
    heads = range(RET_HEADS)
    lanes = [slice(h * HEAD_DIM, (h + 1) * HEAD_DIM) for h in heads]

    def rope(t):
        return t * c + pltpu.roll(t, HEAD_DIM // 2, 1) * s

    q = [rope(q_ref[0, :, lanes[h]]) for h in heads]
    k = [rope(k_ref[0, :, lanes[h]]) * HEAD_DIM ** -0.5 for h in heads]
    v = [v_ref[0, :, lanes[h]] for h in heads]
    st = [state_ref[h] for h in heads]
    scores = [_dot(q[h], k[h], _NT) * intra_ref[h] for h in heads]
    cross = [_dot(q[h] * qdec_ref[h], st[h]) for h in heads]
    inner = [_dot(scores[h], v[h]) for h in heads]
    kv = [_dot(k[h] * kdec_ref[h], v[h], _TN) for h in heads]
    for h in heads:
        o_ref[0, :, lanes[h]] = inner[h] + cross[h]
        state_ref[h] = st[h] * cdec_ref[h] + kv[h]


def retention_scan(p_ret, tab_c, tab_s, n_lat, reverse):
    b, n, _ = p_ret.shape
    nb, nlb = n // RET_CHUNK, n_lat // RET_CHUNK
    consts = _ret_consts(reverse)

    def blk(s):
        return _scan_block(s, nb, nlb, reverse)

    def col(j):
        return pl.BlockSpec((1, RET_CHUNK, RET_W), lambda bi, s: (bi, blk(s), j))

    tab = pl.BlockSpec((RET_CHUNK, HEAD_DIM), lambda bi, s: (blk(s), 0))
    return pl.pallas_call(
        _ret_scan_kernel,
        grid=(b, nb),
        in_specs=[col(0), col(1), col(2), tab, tab]
        + [pl.BlockSpec(a.shape, lambda bi, s: (0, 0, 0)) for a in consts],
        out_specs=pl.BlockSpec((1, RET_CHUNK, RET_W), lambda bi, s: (bi, blk(s), 0)),
        out_shape=jax.ShapeDtypeStruct((b, n, RET_W), F32),
        scratch_shapes=[pltpu.VMEM((RET_HEADS, HEAD_DIM, HEAD_DIM), F32)],
        compiler_params=pltpu.CompilerParams(
            dimension_semantics=("parallel", "arbitrary"), vmem_limit_bytes=VMEM_LIMIT),
    )(p_ret, p_ret, p_ret, tab_c, tab_s, *consts)


def _silu(x):
    return x * jax.nn.sigmoid(x)


def _ret_final_kernel(of_ref, ob_ref, g_ref, w_ref, o_ref):
    for h in range(RET_HEADS):
        lane = slice(h * HEAD_DIM, (h + 1) * HEAD_DIM)
        o = of_ref[:, lane] + ob_ref[:, lane]
        mu = jnp.mean(o, axis=1, keepdims=True)
        d = o - mu
        var = jnp.mean(d * d, axis=1, keepdims=True)
        y = d * lax.rsqrt(var + NORM_EPS) * w_ref[:, lane] * _silu(g_ref[:, lane])
        o_ref[:, lane] = y.astype(o_ref.dtype)


def _gdn_final_kernel(of_ref, ob_ref, g_ref, w_ref, o_ref):
    for h in range(GDN_HEADS):
        lane = slice(h * HEAD_DIM, (h + 1) * HEAD_DIM)
        o = of_ref[:, lane] + ob_ref[:, lane]
        ms = jnp.mean(o * o, axis=1, keepdims=True)
        y = o * lax.rsqrt(ms + NORM_EPS) * w_ref[...] * _silu(g_ref[:, lane])
        o_ref[:, lane] = y.astype(o_ref.dtype)


FINAL_TT = 256


def mixer_finalize(kernel_fn, o_f, o_b, proj, gate_col, w):
    t, wd = o_f.shape
    row = pl.BlockSpec((FINAL_TT, wd), lambda i: (i, 0))
    return pl.pallas_call(
        kernel_fn,
        grid=(t // FINAL_TT,),
        in_specs=[row, row, pl.BlockSpec((FINAL_TT, wd), lambda i: (i, gate_col)),
                  pl.BlockSpec(w.shape, lambda i: (0, 0))],
        out_specs=row,
        out_shape=jax.ShapeDtypeStruct((t, wd), BF16),
        compiler_params=pltpu.CompilerParams(
            dimension_semantics=("parallel",), vmem_limit_bytes=VMEM_LIMIT),
    )(o_f, o_b, proj, w)


def ret_tables(cos, sin, n_ctx):
    c = jnp.concatenate([cos, cos], axis=1)
    s = jnp.concatenate([-sin, sin], axis=1)
    return (jnp.concatenate([c, jnp.ones((n_ctx, HEAD_DIM), F32)], axis=0),
            jnp.concatenate([s, jnp.zeros((n_ctx, HEAD_DIM), F32)], axis=0))


def retention_mixer(p_ret, norm_w, tabs, n_lat):
    b, n, _ = p_ret.shape
    o_f = retention_scan(p_ret, *tabs, n_lat, False)
    o_b = retention_scan(p_ret, *tabs, n_lat, True)
    out = mixer_finalize(_ret_final_kernel, o_f.reshape(b * n, RET_W), o_b.reshape(b * n, RET_W),
                         p_ret.reshape(b * n, 4 * RET_W), 3, norm_w[None, :])
    return out.reshape(b, n, RET_W)


GDN_TT = 256
CONV_HALO = SUBLANES


def _gdn_conv_kernel(main_ref, prev_ref, next_ref, w_ref, o_ref, ext_ref, *, tiles, lat_tiles):
    i = pl.program_id(0) % tiles
    sec = pl.program_id(1)
    seg_first = jnp.logical_or(i == 0, i == lat_tiles)
    seg_last = jnp.logical_or(i == lat_tiles - 1, i == tiles - 1)
    ext_ref[0:CONV_HALO, :] = jnp.where(seg_first, 0.0, prev_ref[...])
    ext_ref[CONV_HALO:CONV_HALO + GDN_TT, :] = main_ref[...]
    ext_ref[CONV_HALO + GDN_TT:, :] = jnp.where(seg_last, 0.0, next_ref[...])
    acc = jnp.zeros((GDN_TT, GDN_W), F32)
    for j in range(CONV_W):
        lo = CONV_HALO - CONV_W // 2 + j
        acc = acc + ext_ref[lo:lo + GDN_TT, :] * w_ref[j:j + 1, :]
    y = _silu(acc)
    scale = jnp.where(sec == 0, HEAD_DIM ** -0.5, 1.0)
    for h in range(GDN_HEADS):
        lane = slice(h * HEAD_DIM, (h + 1) * HEAD_DIM)
        yh = y[:, lane]
        nh = yh * (lax.rsqrt(jnp.sum(yh * yh, axis=1, keepdims=True) + NORM_EPS) * scale)
        o_ref[0, :, lane] = jnp.where(sec == 2, yh, nh)


def gdn_conv(p_gdn, conv_w, n_tot, n_lat):
    t = p_gdn.shape[0]
    tiles, lat_tiles = n_tot // GDN_TT, n_lat // GDN_TT
    per = GDN_TT // CONV_HALO
    last = t // CONV_HALO - 1
    return pl.pallas_call(
        functools.partial(_gdn_conv_kernel, tiles=tiles, lat_tiles=lat_tiles),
        grid=(t // GDN_TT, 3),
        in_specs=[pl.BlockSpec((GDN_TT, GDN_W), lambda i, s: (i, s)),
                  pl.BlockSpec((CONV_HALO, GDN_W), lambda i, s: (jnp.maximum(i * per - 1, 0), s)),
                  pl.BlockSpec((CONV_HALO, GDN_W), lambda i, s: (jnp.minimum((i + 1) * per, last), s)),
                  pl.BlockSpec((CONV_W, GDN_W), lambda i, s: (0, s))],
        out_specs=pl.BlockSpec((1, GDN_TT, GDN_W), lambda i, s: (s, i, 0)),
        out_shape=jax.ShapeDtypeStruct((3, t, GDN_W), F32),
        scratch_shapes=[pltpu.VMEM((GDN_TT + 2 * CONV_HALO, GDN_W), F32)],
        compiler_params=pltpu.CompilerParams(
            dimension_semantics=("parallel", "parallel"), vmem_limit_bytes=VMEM_LIMIT),
    )(p_gdn, p_gdn, p_gdn, conv_w)


def _gdn_gate_kernel(a_ref, alog_ref, dt_ref, sel_ref, gf_ref, gb_ref, bf_ref, bb_ref):
    x = a_ref[...]
    lane = lax.broadcasted_iota(jnp.int32, x.shape, 1)
    g = -jnp.exp(alog_ref[...]) * jax.nn.softplus(x + dt_ref[...])
    gate = jnp.where(lane < 2 * GDN_HEADS, g, jax.nn.sigmoid(x))
    for idx, ref in enumerate((gf_ref, gb_ref, bf_ref, bb_ref)):
        ref[...] = _dot(gate, sel_ref[idx], precision=lax.Precision.HIGHEST)


def gdn_gates(p_sml, a_log, dt_bias):
    t = p_sml.shape[0]
    pad = LANES - 2 * GDN_HEADS
    alog = jnp.pad(a_log.reshape(1, 2 * GDN_HEADS).astype(F32), ((0, 0), (0, pad)))
    dt = jnp.pad(dt_bias.reshape(1, 2 * GDN_HEADS).astype(F32), ((0, 0), (0, pad)))
    sel = np.zeros((4, LANES, GDN_W), np.float32)
    for idx in range(4):
        for h in range(GDN_HEADS):
            sel[idx, idx * GDN_HEADS + h, h * HEAD_DIM:(h + 1) * HEAD_DIM] = 1.0
    wide = jax.ShapeDtypeStruct((t, GDN_W), F32)
    wide_spec = pl.BlockSpec((GDN_TT, GDN_W), lambda i: (i, 0))
    vec_spec = pl.BlockSpec((1, LANES), lambda i: (0, 0))
    return pl.pallas_call(
        _gdn_gate_kernel,
        grid=(t // GDN_TT,),
        in_specs=[pl.BlockSpec((GDN_TT, LANES), lambda i: (i, 0)), vec_spec, vec_spec,
                  pl.BlockSpec(sel.shape, lambda i: (0, 0, 0))],
        out_specs=[wide_spec] * 4,
        out_shape=[wide] * 4,
        compiler_params=pltpu.CompilerParams(
            dimension_semantics=("parallel",), vmem_limit_bytes=VMEM_LIMIT),
    )(p_sml, alog, dt, jnp.asarray(sel))


def _gdn_scan_kernel(q_ref, k_ref, v_ref, g_ref, b_ref, o_ref, state_ref, *, reverse):
    @pl.when(pl.program_id(1) == 0)
    def _():
        state_ref[...] = jnp.zeros_like(state_ref)

    cl = GDN_CHUNK
    hi = lax.Precision.HIGHEST
    r = lax.broadcasted_iota(jnp.int32, (cl, cl), 0)
    c = lax.broadcasted_iota(jnp.int32, (cl, cl), 1)
    inc = (r <= c) if reverse else (r >= c)
    inc_f = inc.astype(F32)
    inc_t = ((c <= r) if reverse else (c >= r)).astype(F32)
    eye = (r == c).astype(F32)
    strict_f = inc_f - eye
    ones = jnp.ones((cl, HEAD_DIM), F32)
    n_chunks = SCAN_ROWS // cl
    order = range(n_chunks - 1, -1, -1) if reverse else range(n_chunks)
    heads = range(GDN_HEADS)
    items = [(h, ci) for ci in order for h in heads]

    def load(ref, lead):
        return [ref[lead + (slice(ci * cl, (ci + 1) * cl), slice(h * HEAD_DIM, (h + 1) * HEAD_DIM))]
                for h, ci in items]

    q, k, v = load(q_ref, (0, 0)), load(k_ref, (0, 0)), load(v_ref, (0, 0))
    gc, bc = load(g_ref, (0,)), load(b_ref, (0,))
    n = range(len(items))
    gcum = [_dot(inc_f, gc[i], precision=hi) for i in n]
    g_rows = [_dot(ones, gc[i], _NT, precision=hi) * (1.0 / HEAD_DIM) for i in n]
    gcum_row = [_dot(g_rows[i], inc_t, precision=hi) for i in n]
    gtot = [jnp.sum(gc[i], axis=0, keepdims=True) for i in n]
    decay = [jnp.where(inc, jnp.exp(jnp.where(inc, gcum[i][:, :cl] - gcum_row[i], 0.0)), 0.0) for i in n]
    eg = [jnp.exp(gcum[i]) for i in n]
    kb = [k[i] * bc[i] for i in n]
    a_pow = [_dot(kb[i], k[i], _NT) * decay[i] * strict_f for i in n]
    t_mat = [eye - a_pow[i] for i in n]
    for _ in range(int(math.log2(cl)) - 1):
        a_pow = [_dot(a_pow[i], a_pow[i]) for i in n]
        t_mat = [t_mat[i] + _dot(t_mat[i], a_pow[i]) for i in n]
    u = [_dot(t_mat[i], v[i] * bc[i]) for i in n]
    w = [_dot(t_mat[i], kb[i] * eg[i]) for i in n]
    attn = [_dot(q[i], k[i], _NT) * decay[i] for i in n]
    qd = [q[i] * eg[i] for i in n]
    kd = [k[i] * jnp.exp(gtot[i] - gcum[i]) for i in n]
    cd = [jnp.exp(gtot[i]) for i in n]
    st = [state_ref[h] for h in heads]
    for pos, ci in enumerate(order):
        idx = [pos * GDN_HEADS + h for h in heads]
        v_new = [u[i] - _dot(w[i], st[h]) for h, i in zip(heads, idx)]
        out = [_dot(qd[i], st[h]) + _dot(attn[i], v_new[h]) for h, i in zip(heads, idx)]
        st = [st[h] * cd[i] + _dot(kd[i], v_new[h], _TN) for h, i in zip(heads, idx)]
        for h in heads:
            o_ref[0, ci * cl:(ci + 1) * cl, h * HEAD_DIM:(h + 1) * HEAD_DIM] = out[h]
    for h in heads:
        state_ref[h] = st[h]


def gdn_scan(qkv, g, beta, n_lat, reverse):
    _, b, n, _ = qkv.shape
    nb, nlb = n // SCAN_ROWS, n_lat // SCAN_ROWS

    def blk(s):
        return _scan_block(s, nb, nlb, reverse)

    def sec(j):
        return pl.BlockSpec((1, 1, SCAN_ROWS, GDN_W), lambda bi, s: (j, bi, blk(s), 0))

    row = pl.BlockSpec((1, SCAN_ROWS, GDN_W), lambda bi, s: (bi, blk(s), 0))
    return pl.pallas_call(
        functools.partial(_gdn_scan_kernel, reverse=reverse),
        grid=(b, nb),
        in_specs=[sec(0), sec(1), sec(2), row, row],
        out_specs=row,
        out_shape=jax.ShapeDtypeStruct((b, n, GDN_W), F32),
        scratch_shapes=[pltpu.VMEM((GDN_HEADS, HEAD_DIM, HEAD_DIM), F32)],
        compiler_params=pltpu.CompilerParams(
            dimension_semantics=("parallel", "arbitrary"), vmem_limit_bytes=VMEM_LIMIT),
    )(qkv, qkv, qkv, g, beta)


def gdn_mixer(p_gdn, p_sml, lp, n_lat):
    b, n, _ = p_gdn.shape
    p_gdn2 = p_gdn.reshape(b * n, 4 * GDN_W)
    qkv = gdn_conv(p_gdn2, lp['conv_w'], n, n_lat).reshape(3, b, n, GDN_W)
    g_f, g_b, be_f, be_b = (a.reshape(b, n, GDN_W) for a in
                            gdn_gates(p_sml.reshape(b * n, -1), lp['gdn_a_log'], lp['gdn_dt_bias']))
    o_f = gdn_scan(qkv, g_f, be_f, n_lat, False)
    o_b = gdn_scan(qkv, g_b, be_b, n_lat, True)
    out = mixer_finalize(_gdn_final_kernel, o_f.reshape(b * n, GDN_W), o_b.reshape(b * n, GDN_W),
                         p_gdn2, 3, lp['gdn_norm_w'][None, :])
    return out.reshape(b, n, GDN_W)


def rms_norm(x, w):
    xf = x.astype(F32)
    y = xf * lax.rsqrt(jnp.mean(xf * xf, axis=-1, keepdims=True) + NORM_EPS)
    return (y * w.astype(F32)).astype(x.dtype)


def modulate(h, shift, scale):
    return h * (1 + scale) + shift


def split_cols(t, sizes):
    parts, start = [], 0
    for s in sizes:
        parts.append(t[..., start:start + s])
        start += s
    return parts


def axial_rope_tables(row, col, rot_dim):
    n_freq = rot_dim // 4
    inv_freq = ROPE_BASE ** (-jnp.arange(n_freq, dtype=F32) / n_freq)
    ang = jnp.concatenate([row[:, None] * inv_freq, col[:, None] * inv_freq], axis=-1)
    return jnp.cos(ang), jnp.sin(ang)


MLA_HP = 2 * LANES
MLA_TT = 256
MLA_TQ = 512
MLA_TK = 2048
MLA_Q_SCALE = MLA_QK ** -0.5 * math.log2(math.e)


def _rope_tail(tail, c, s1, s2):
    half = ROPE_DIM // 2
    return tail * c + pltpu.roll(tail, LANES - half, 1) * s1 + pltpu.roll(tail, half, 1) * s2


def _mla_q_prep_kernel(x_ref, w_ref, c_ref, s1_ref, s2_ref, o_ref):
    c, s1, s2 = c_ref[...], s1_ref[...], s2_ref[...]
    for h in range(MLA_HEADS):
        lo = h * MLA_HP
        x = x_ref[:, lo:lo + MLA_HP]
        ms = jnp.sum(x * x, axis=1, keepdims=True) * (1.0 / MLA_QK)
        y = x * lax.rsqrt(ms + NORM_EPS) * w_ref[...]
        tail = _rope_tail(y[:, NOPE_DIM:], c, s1, s2)
        o_ref[:, lo:lo + NOPE_DIM] = (y[:, :NOPE_DIM] * MLA_Q_SCALE).astype(o_ref.dtype)
        o_ref[:, lo + NOPE_DIM:lo + MLA_HP] = (tail * MLA_Q_SCALE).astype(o_ref.dtype)


def _mla_kv_prep_kernel(kv_ref, kr_ref, w_ref, c_ref, s1_ref, s2_ref, k_ref, v_ref):
    c, s1, s2 = c_ref[...], s1_ref[...], s2_ref[...]
    kr = kr_ref[...]
    kr_ss = jnp.sum(kr * kr, axis=1, keepdims=True)
    for h in range(MLA_HEADS):
        lo = h * MLA_HP
        nope = kv_ref[:, lo:lo + NOPE_DIM]
        ms = (jnp.sum(nope * nope, axis=1, keepdims=True) + kr_ss) * (1.0 / MLA_QK)
        r = lax.rsqrt(ms + NORM_EPS)
        k_ref[:, lo:lo + NOPE_DIM] = (nope * r * w_ref[:, :NOPE_DIM]).astype(k_ref.dtype)
        tail = _rope_tail(kr * r * w_ref[:, NOPE_DIM:], c, s1, s2)
        k_ref[:, lo + NOPE_DIM:lo + MLA_HP] = tail.astype(k_ref.dtype)
        v_ref[:, h * V_DIM:(h + 1) * V_DIM] = kv_ref[:, lo + NOPE_DIM:lo + NOPE_DIM + V_DIM].astype(v_ref.dtype)


def mla_prep(qraw, kvraw, kr_pad, qw_pad, kw_pad, tabs, n_tot):
    t = qraw.shape[0]
    tiles = n_tot // MLA_TT
    tab_spec = pl.BlockSpec((MLA_TT, LANES), lambda i: (i % tiles, 0))
    w_spec = pl.BlockSpec((1, MLA_HP), lambda i: (0, 0))
    wide_spec = pl.BlockSpec((MLA_TT, MLA_HEADS * MLA_HP), lambda i: (i, 0))
    params = pltpu.CompilerParams(dimension_semantics=("parallel",), vmem_limit_bytes=VMEM_LIMIT)
    q = pl.pallas_call(
        _mla_q_prep_kernel,
        grid=(t // MLA_TT,),
        in_specs=[wide_spec, w_spec, tab_spec, tab_spec, tab_spec],
        out_specs=wide_spec,
        out_shape=jax.ShapeDtypeStruct((t, MLA_HEADS * MLA_HP), BF16),
        compiler_params=params,
    )(qraw, qw_pad, *tabs)
    k, v = pl.pallas_call(
        _mla_kv_prep_kernel,
        grid=(t // MLA_TT,),
        in_specs=[wide_spec, pl.BlockSpec((MLA_TT, LANES), lambda i: (i, 0)), w_spec,
                  tab_spec, tab_spec, tab_spec],
        out_specs=[wide_spec, pl.BlockSpec((MLA_TT, MLA_HEADS * V_DIM), lambda i: (i, 0))],
        out_shape=[jax.ShapeDtypeStruct((t, MLA_HEADS * MLA_HP), BF16),
                   jax.ShapeDtypeStruct((t, MLA_HEADS * V_DIM), BF16)],
        compiler_params=params,
    )(kvraw, kr_pad, kw_pad, *tabs)
    return q, k, v


def _mla_flash_kernel(q_ref, k_ref, v_ref, o_ref, *, n_loop, tail_len):
    q = q_ref[0]
    tq = q.shape[0]

    def attend(kc, vc, carry):
        m, l, acc = carry
        s = lax.dot_general(q, kc, (((1,), (1,)), ((), ())), preferred_element_type=F32)
        m_new = jnp.maximum(m, jnp.max(s, axis=1, keepdims=True))
        alpha = jnp.exp2(m - m_new)
        p = jnp.exp2(s - m_new)
        l = alpha * l + jnp.sum(p, axis=1, keepdims=True)
        acc = alpha * acc + jnp.dot(p.astype(vc.dtype), vc, preferred_element_type=F32)
        return m_new, l, acc

    def body(c, carry):
        off = pl.multiple_of(c * MLA_TK, MLA_TK)
        return attend(k_ref[0, pl.ds(off, MLA_TK), :], v_ref[0, pl.ds(off, MLA_TK), :], carry)

    carry = (jnp.full((tq, 1), -jnp.inf, F32), jnp.zeros((tq, 1), F32), jnp.zeros((tq, V_DIM), F32))
    if n_loop:
        carry = lax.fori_loop(0, n_loop, body, carry)
    if tail_len:
        lo = n_loop * MLA_TK
        carry = attend(k_ref[0, lo:lo + tail_len, :], v_ref[0, lo:lo + tail_len, :], carry)
    _, l, acc = carry
    o_ref[0] = (acc / l).astype(o_ref.dtype)


def mla_attend(q, k, v, n_lat, n_ctx, context_queries):
    b = q.shape[0]
    params = pltpu.CompilerParams(dimension_semantics=("parallel", "parallel", "arbitrary"),
                                  vmem_limit_bytes=VMEM_LIMIT)
    if context_queries:
        blk = n_lat // n_ctx
        return pl.pallas_call(
            functools.partial(_mla_flash_kernel, n_loop=0, tail_len=n_ctx),
            grid=(b, MLA_HEADS, 1),
            in_specs=[pl.BlockSpec((1, n_ctx, MLA_HP), lambda bi, h, i: (bi, blk, h)),
                      pl.BlockSpec((1, n_ctx, MLA_HP), lambda bi, h, i: (bi, blk, h)),
                      pl.BlockSpec((1, n_ctx, V_DIM), lambda bi, h, i: (bi, blk, h))],
            out_specs=pl.BlockSpec((1, n_ctx, V_DIM), lambda bi, h, i: (bi, 0, h)),
            out_shape=jax.ShapeDtypeStruct((b, n_ctx, MLA_HEADS * V_DIM), BF16),
            compiler_params=params,
        )(q, k, v)
    n_tot = n_lat + n_ctx
    assert n_lat % MLA_TK == 0 and n_lat % MLA_TQ == 0
    return pl.pallas_call(
        functools.partial(_mla_flash_kernel, n_loop=n_lat // MLA_TK, tail_len=n_ctx),
        grid=(b, MLA_HEADS, n_lat // MLA_TQ),
        in_specs=[pl.BlockSpec((1, MLA_TQ, MLA_HP), lambda bi, h, i: (bi, i, h)),
                  pl.BlockSpec((1, n_tot, MLA_HP), lambda bi, h, i: (bi, 0, h)),
                  pl.BlockSpec((1, n_tot, V_DIM), lambda bi, h, i: (bi, 0, h))],
        out_specs=pl.BlockSpec((1, MLA_TQ, V_DIM), lambda bi, h, i: (bi, i, h)),
        out_shape=jax.ShapeDtypeStruct((b, n_lat, MLA_HEADS * V_DIM), BF16),
        compiler_params=params,
    )(q, k, v)


def mla_tables(cos, sin, n_ctx):
    n_lat, half = cos.shape
    z = jnp.zeros((n_lat, LANES - 2 * half), F32)
    zh = jnp.zeros((n_lat, half), F32)
    c = jnp.concatenate([cos, cos, z], axis=1)
    s1 = jnp.concatenate([-sin, zh, z], axis=1)
    s2 = jnp.concatenate([zh, sin, z], axis=1)
    ident = jnp.concatenate([jnp.ones((n_ctx, 2 * half), F32), jnp.zeros((n_ctx, LANES - 2 * half), F32)], axis=1)
    zero = jnp.zeros((n_ctx, LANES), F32)
    return (jnp.concatenate([c, ident], axis=0), jnp.concatenate([s1, zero], axis=0),
            jnp.concatenate([s2, zero], axis=0))


def mla_mixer(c_q, c_kv, k_rope, lp, tabs, n_lat, with_ctx):
    b, n, _ = c_q.shape
    n_ctx = n - n_lat
    qraw = matmul(rms_norm(c_q, lp['cq_norm_w']).reshape(b * n, Q_LORA).astype(BF16), lp['w_uq_pad'])
    kvraw = matmul(rms_norm(c_kv, lp['ckv_norm_w']).reshape(b * n, KV_LORA).astype(BF16), lp['w_ukv'])
    kr_pad = jnp.pad(k_rope.reshape(b * n, ROPE_DIM), ((0, 0), (0, LANES - ROPE_DIM)))
    q, k, v = mla_prep(qraw, kvraw, kr_pad, lp['q_norm_pad'], lp['k_norm_pad'], tabs, n)
    q = q.reshape(b, n, -1)
    k = k.reshape(b, n, -1)
    v = v.reshape(b, n, -1)
    o_x = mla_attend(q, k, v, n_lat, n_ctx, False)
    if not with_ctx:
        return o_x
    return jnp.concatenate([o_x, mla_attend(q, k, v, n_lat, n_ctx, True)], axis=1)


def trunk_layer(x, ctx, mod_x, mod_c, lp, ret_tabs, mla_tabs, update_ctx):
    b, n_ctx, d = ctx.shape
    n_lat = x.shape[1]
    n_tot = n_ctx + n_lat
    sh1, sc1, g1, sh2, sc2, g2 = (m[:, None, :] for m in jnp.split(mod_x, 6, axis=-1))
    csh1, csc1, cg1, csh2, csc2, cg2 = jnp.split(mod_c, 6, axis=-1)

    h = jnp.concatenate([modulate(rms_norm(x, lp['norm1_w']), sh1, sc1),
                         modulate(rms_norm(ctx, lp['norm1_w']), csh1, csc1)], axis=1)
    h2 = h.reshape(b * n_tot, d).astype(BF16)
    p_ret = matmul(h2, lp['w_in_ret']).reshape(b, n_tot, 4 * RET_W)
    p_gdn = matmul(h2, lp['w_in_gdn']).reshape(b, n_tot, 4 * GDN_W)
    p_sml = matmul(h2, lp['w_in_sml'], tn=2048).reshape(b, n_tot, -1)
    _, _, _, _, c_q, c_kv, k_rope = split_cols(p_sml, (GDN_HEADS,) * 4 + (Q_LORA, KV_LORA, ROPE_DIM))
    ret_o = retention_mixer(p_ret, lp['ret_norm_w'], ret_tabs, n_lat)
    gdn_o = gdn_mixer(p_gdn, p_sml, lp, n_lat)
    mla_o = mla_mixer(c_q, c_kv, k_rope, lp, mla_tabs, n_lat, update_ctx)
    if update_ctx:
        mixed = jnp.concatenate([ret_o, gdn_o, mla_o], axis=-1)
        y = matmul(mixed.reshape(b * n_tot, MIX_W), lp['w_out']).reshape(b, n_tot, d)
        x = x + g1 * y[:, :n_lat]
        ctx = ctx + cg1 * y[:, n_lat:]
        f = jnp.concatenate([modulate(rms_norm(x, lp['norm2_w']), sh2, sc2),
                             modulate(rms_norm(ctx, lp['norm2_w']), csh2, csc2)], axis=1)
        f = peer_ffn(f.reshape(b * n_tot, d).astype(BF16), lp['peer_wq'], lp['peer_keys'], lp['peer_u'],
                     lp['peer_vt']).reshape(b, n_tot, d)
        x = x + g2 * f[:, :n_lat]
        ctx = ctx + cg2 * f[:, n_lat:]
    else:
        mixed = jnp.concatenate([ret_o[:, :n_lat], gdn_o[:, :n_lat], mla_o], axis=-1)
        y = matmul(mixed.reshape(b * n_lat, MIX_W), lp['w_out']).reshape(b, n_lat, d)
        x = x + g1 * y
        f = modulate(rms_norm(x, lp['norm2_w']), sh2, sc2)
        f = peer_ffn(f.reshape(b * n_lat, d).astype(BF16), lp['peer_wq'], lp['peer_keys'], lp['peer_u'],
                     lp['peer_vt'])
        x = x + g2 * f.reshape(b, n_lat, d)
    return x, ctx


def kernel(x, c, ctx, c_ctx, ada_w, ada_b, norm1_w, norm2_w, w_in, conv_w, gdn_a_log, gdn_dt_bias,
           gdn_norm_w, ret_norm_w, cq_norm_w, ckv_norm_w, w_uq, w_ukv, q_norm_w, k_norm_w, w_out,
           peer_wq, peer_keys, peer_u, peer_v):
    n_lat = x.shape[1]
    n_rows = n_lat // GRID_W
    row = jnp.repeat(jnp.arange(n_rows, dtype=F32), GRID_W, total_repeat_length=n_lat)
    col = (jnp.arange(n_lat) % GRID_W).astype(F32)
    ret_tabs = ret_tables(*axial_rope_tables(row, col, HEAD_DIM), ctx.shape[1])
    mla_tabs = mla_tables(*axial_rope_tables(row, col, ROPE_DIM), ctx.shape[1])
    silu_c = jax.nn.silu(c)
    silu_cc = jax.nn.silu(c_ctx)
    sml_w = 4 * GDN_HEADS + Q_LORA + KV_LORA + ROPE_DIM
    sml_pad = -sml_w % LANES
    for i in range(DEPTH):
        mod_x = silu_c @ ada_w[i] + ada_b[i]
        mod_c = silu_cc @ ada_w[i] + ada_b[i]
        w_in_i = w_in[i].astype(BF16)
        lp = {
            'norm1_w': norm1_w[i], 'norm2_w': norm2_w[i], 'conv_w': conv_w[i],
            'w_in_ret': w_in_i[:, :4 * RET_W],
            'w_in_gdn': w_in_i[:, 4 * RET_W:4 * RET_W + 4 * GDN_W],
            'w_in_sml': jnp.pad(w_in_i[:, 4 * RET_W + 4 * GDN_W:], ((0, 0), (0, sml_pad))),
            'gdn_a_log': gdn_a_log[i], 'gdn_dt_bias': gdn_dt_bias[i], 'gdn_norm_w': gdn_norm_w[i],
            'ret_norm_w': ret_norm_w[i], 'cq_norm_w': cq_norm_w[i], 'ckv_norm_w': ckv_norm_w[i],
            'w_uq_pad': jnp.pad(w_uq[i].astype(BF16).reshape(Q_LORA, MLA_HEADS, MLA_QK),
                                ((0, 0), (0, 0), (0, MLA_HP - MLA_QK))).reshape(Q_LORA, MLA_HEADS * MLA_HP),
            'w_ukv': w_ukv[i].astype(BF16),
            'q_norm_pad': jnp.pad(q_norm_w[i], (0, MLA_HP - MLA_QK))[None, :],
            'k_norm_pad': jnp.pad(k_norm_w[i], (0, MLA_HP - MLA_QK))[None, :],
            'w_out': w_out[i].astype(BF16), 'peer_wq': peer_wq[i].astype(BF16), 'peer_keys': peer_keys[i],
            'peer_u': peer_u[i].astype(BF16), 'peer_vt': peer_v[i].astype(BF16).T,
        }
        x, ctx = trunk_layer(x, ctx, mod_x, mod_c, lp, ret_tabs, mla_tabs, i < DEPTH - 1)
    return x
```

```python
import functools
import math

import jax
import jax.numpy as jnp
import numpy as np
from jax import lax
from jax.experimental import pallas as pl
from jax.experimental.pallas import tpu as pltpu

D_MODEL = 4096
DEPTH = 2
GRID_W = 64
HEAD_DIM = 128
RET_HEADS = 8
RET_CHUNK = 128
GDN_HEADS = 8
GDN_CHUNK = 64
CONV_W = 5
MLA_HEADS = 16
Q_LORA = 1024
KV_LORA = 512
NOPE_DIM = 128
ROPE_DIM = 64
V_DIM = 128
MLA_QK = NOPE_DIM + ROPE_DIM
ATTN_BLOCK = 128
PEER_HEADS = 8
N_KEYS = 128
N_EXPERTS = N_KEYS * N_KEYS
PEER_TOPK = 16
PEER_QDIM = 256
ROPE_BASE = 10000.0
NORM_EPS = 1e-6
RET_W = RET_HEADS * HEAD_DIM
GDN_W = GDN_HEADS * HEAD_DIM
MLA_W = MLA_HEADS * V_DIM
MIX_W = RET_W + GDN_W + MLA_W
IN_SIZES = (RET_W,) * 4 + (GDN_W,) * 4 + (GDN_HEADS,) * 4 + (Q_LORA, KV_LORA, ROPE_DIM)

LANES = 128
SUBLANES = 8
VMEM_LIMIT = 56 * 1024 * 1024

F32 = jnp.float32
BF16 = jnp.bfloat16


def _mm_kernel(a_ref, b_ref, o_ref):
    o_ref[...] = jnp.dot(a_ref[...].astype(BF16), b_ref[...].astype(BF16),
                         preferred_element_type=F32).astype(o_ref.dtype)


def _pick_tile(n, target, align):
    best = n
    for t in range(align, min(n, target) + 1, align):
        if n % t == 0:
            best = t
    return best


def matmul(a, b, out_dtype=F32, tm=512, tn=512):
    m, k = a.shape
    n = b.shape[1]
    tm = _pick_tile(m, tm, 8)
    tn = _pick_tile(n, tn, LANES)
    return pl.pallas_call(
        _mm_kernel,
        grid=(m // tm, n // tn),
        in_specs=[pl.BlockSpec((tm, k), lambda i, j: (i, 0)),
                  pl.BlockSpec((k, tn), lambda i, j: (0, j))],
        out_specs=pl.BlockSpec((tm, tn), lambda i, j: (i, j)),
        out_shape=jax.ShapeDtypeStruct((m, n), out_dtype),
        compiler_params=pltpu.CompilerParams(
            dimension_semantics=("parallel", "parallel"), vmem_limit_bytes=VMEM_LIMIT),
    )(a, b)


PEER_TT = 512
PEER_EB = 512
PEER_JB = PEER_EB // N_KEYS
assert SUBLANES == 2 * PEER_JB


def _gelu(a):
    return 0.5 * a * (1.0 + lax.erf(a * (2.0 ** -0.5)))


def _peer_kernel(x_ref, u_ref, vt_ref, s1_ref, e1_ref, s2_ref, e2_ref, tau_ref, o_ref, acc_ref):
    e = pl.program_id(1)

    @pl.when(e == 0)
    def _():
        acc_ref[...] = jnp.zeros_like(acc_ref)

    act_t = lax.dot_general(u_ref[...], x_ref[...], (((1,), (1,)), ((), ())),
                            preferred_element_type=F32)
    upper = (e % 2) == 1

    def key_rows(ref, h, lane):
        blk = ref[h, :, lane]
        return jnp.where(upper, pltpu.roll(blk, PEER_JB, 0), blk)

    rows = []
    for jj in range(PEER_JB):
        cols = []
        for ts in range(PEER_TT // LANES):
            lane = slice(ts * LANES, (ts + 1) * LANES)
            g = jnp.zeros((N_KEYS, LANES), F32)
            for h in range(PEER_HEADS):
                s1row = key_rows(s1_ref, h, lane)[jj:jj + 1, :]
                e1row = key_rows(e1_ref, h, lane)[jj:jj + 1, :]
                t = s2_ref[h, :, lane] + s1row
                g = g + jnp.where(t >= tau_ref[h:h + 1, lane], e2_ref[h, :, lane] * e1row, 0.0)
            a = act_t[jj * N_KEYS:(jj + 1) * N_KEYS, lane]
            cols.append((_gelu(a) * g).astype(BF16))
        rows.append(jnp.concatenate(cols, axis=1))
    c_t = jnp.concatenate(rows, axis=0)
    acc_ref[...] += jnp.dot(vt_ref[...], c_t, preferred_element_type=F32)

    @pl.when(e == pl.num_programs(1) - 1)
    def _():
        o_ref[...] = acc_ref[...].T.astype(o_ref.dtype)


def peer_dense(x, u, vt, s1, e1, s2, e2, tau):
    t, d = x.shape
    n_e = u.shape[0] // PEER_EB
    row_spec = pl.BlockSpec((PEER_HEADS, SUBLANES, PEER_TT),
                            lambda i, e: (0, e // (SUBLANES // PEER_JB), i))
    return pl.pallas_call(
        _peer_kernel,
        grid=(t // PEER_TT, n_e),
        in_specs=[
            pl.BlockSpec((PEER_TT, d), lambda i, e: (i, 0)),
            pl.BlockSpec((PEER_EB, d), lambda i, e: (e, 0)),
            pl.BlockSpec((d, PEER_EB), lambda i, e: (0, e)),
            row_spec,
            row_spec,
            pl.BlockSpec((PEER_HEADS, N_KEYS, PEER_TT), lambda i, e: (0, 0, i)),
            pl.BlockSpec((PEER_HEADS, N_KEYS, PEER_TT), lambda i, e: (0, 0, i)),
            pl.BlockSpec((PEER_HEADS, PEER_TT), lambda i, e: (0, i)),
        ],
        out_specs=pl.BlockSpec((PEER_TT, d), lambda i, e: (i, 0)),
        out_shape=jax.ShapeDtypeStruct((t, d), BF16),
        scratch_shapes=[pltpu.VMEM((d, PEER_TT), F32)],
        compiler_params=pltpu.CompilerParams(
            dimension_semantics=("parallel", "arbitrary"), vmem_limit_bytes=VMEM_LIMIT),
    )(x, u, vt, s1, e1, s2, e2, tau)


def _extract_top(work_ref, top_ref, n_groups):
    n_rows = work_ref.shape[1]
    iota = lax.broadcasted_iota(jnp.int32, (n_rows, LANES), 0)

    def body(r, carry):
        for g in range(n_groups):
            w = work_ref[g]
            m = jnp.max(w, axis=0, keepdims=True)
            top_ref[g, pl.ds(r, 1), :] = m
            first = jnp.min(jnp.where(w == m, iota, n_rows), axis=0, keepdims=True)
            work_ref[g] = jnp.where(iota == first, -jnp.inf, w)
        return carry

    lax.fori_loop(0, PEER_TOPK, body, 0)


def _peer_select_kernel(q_ref, keys_ref, s1_ref, e1_ref, s2_ref, e2_ref, tau_ref,
                        work_ref, top_ref, cand_ref, cwork_ref, ctop_ref):
    for h in range(PEER_HEADS):
        for p in range(2):
            g = 2 * h + p
            sc = lax.dot_general(keys_ref[h, p], q_ref[:, g * N_KEYS:(g + 1) * N_KEYS],
                                 (((1,), (1,)), ((), ())), preferred_element_type=F32,
                                 precision=lax.Precision.HIGHEST)
            work_ref[g] = sc
            (s1_ref, s2_ref)[p][h] = sc
    _extract_top(work_ref, top_ref, 2 * PEER_HEADS)
    half = PEER_TOPK // 2
    for h in range(PEER_HEADS):
        t1 = top_ref[2 * h]
        t2 = top_ref[2 * h + 1]
        pieces = [t1[0:1, :] + t2]
        pieces += [t1[p:p + 1, :] + t2[:half, :] for p in range(1, half)]
        pieces += [t1[half:, :] + t2[0:1, :]]
        lo = 0
        for c in pieces:
            cand_ref[h, lo:lo + c.shape[0], :] = c
            cwork_ref[h, lo:lo + c.shape[0], :] = c
            lo += c.shape[0]
    _extract_top(cwork_ref, ctop_ref, PEER_HEADS)
    for h in range(PEER_HEADS):
        tau = ctop_ref[h, PEER_TOPK - 1:PEER_TOPK, :]
        m1 = top_ref[2 * h, 0:1, :]
        m2 = top_ref[2 * h + 1, 0:1, :]
        cand = cand_ref[h]
        z = jnp.sum(jnp.where(cand >= tau, jnp.exp(cand - (m1 + m2)), 0.0), axis=0, keepdims=True)
        tau_ref[h:h + 1, :] = tau
        e1_ref[h] = jnp.exp(s1_ref[h] - m1)
        e2_ref[h] = jnp.exp(s2_ref[h] - m2) / z


def peer_select(q, keys):
    t = q.shape[0]
    big = jax.ShapeDtypeStruct((PEER_HEADS, N_KEYS, t), F32)
    big_spec = pl.BlockSpec((PEER_HEADS, N_KEYS, LANES), lambda i: (0, 0, i))
    n_cand = PEER_TOPK + (PEER_TOPK // 2 - 1) * (PEER_TOPK // 2) + PEER_TOPK // 2
    return pl.pallas_call(
        _peer_select_kernel,
        grid=(t // LANES,),
        in_specs=[pl.BlockSpec((LANES, q.shape[1]), lambda i: (i, 0)),
                  pl.BlockSpec(keys.shape, lambda i: (0, 0, 0, 0))],
        out_specs=[big_spec, big_spec, big_spec, big_spec,
                   pl.BlockSpec((PEER_HEADS, LANES), lambda i: (0, i))],
        out_shape=[big, big, big, big, jax.ShapeDtypeStruct((PEER_HEADS, t), F32)],
        scratch_shapes=[pltpu.VMEM((2 * PEER_HEADS, N_KEYS, LANES), F32),
                        pltpu.VMEM((2 * PEER_HEADS, PEER_TOPK, LANES), F32),
                        pltpu.VMEM((PEER_HEADS, n_cand, LANES), F32),
                        pltpu.VMEM((PEER_HEADS, n_cand, LANES), F32),
                        pltpu.VMEM((PEER_HEADS, PEER_TOPK, LANES), F32)],
        compiler_params=pltpu.CompilerParams(
            dimension_semantics=("parallel",), vmem_limit_bytes=VMEM_LIMIT),
    )(q, keys)


def peer_ffn(f_bf, wq, sub_keys, u_bf, vt_bf):
    q = matmul(f_bf, wq, F32)
    s1, e1, s2, e2, tau = peer_select(q, sub_keys)
    return peer_dense(f_bf, u_bf, vt_bf, s1, e1, s2, e2, tau)


SCAN_ROWS = 128


def _scan_block(step, n_blocks, n_lat_blocks, reverse):
    return (n_blocks - 1 - step) if reverse else (step + n_lat_blocks) % n_blocks


def _dot(a, b, dims=(((1,), (0,)), ((), ())), precision=None):
    if precision is None:
        a, b = a.astype(BF16), b.astype(BF16)
    return lax.dot_general(a, b, dims, preferred_element_type=F32, precision=precision)


_NT = (((1,), (1,)), ((), ()))
_TN = (((0,), (0,)), ((), ()))


RET_LOG_GAMMA = [math.log1p(-(2.0 ** (-5.0 - h))) for h in range(RET_HEADS)]


def _ret_consts(reverse):
    lg = np.array(RET_LOG_GAMMA[::-1] if reverse else RET_LOG_GAMMA, np.float64)[:, None, None]
    pos = np.arange(RET_CHUNK, dtype=np.float64)
    diff = pos[:, None] - pos[None, :]
    rank = pos
    if reverse:
        diff, rank = -diff, RET_CHUNK - 1.0 - pos
    ones = np.ones((1, 1, HEAD_DIM))
    intra = np.where(diff >= 0, np.exp(lg * np.maximum(diff, 0.0)), 0.0)
    qdec = np.exp(lg * (rank + 1.0)[None, :, None]) * ones
    kdec = np.exp(lg * (RET_CHUNK - 1.0 - rank)[None, :, None]) * ones
    cdec = np.exp(lg * RET_CHUNK) * ones
    return [jnp.asarray(a, F32) for a in (intra, qdec, kdec, cdec)]


def _ret_scan_kernel(q_ref, k_ref, v_ref, c_ref, s_ref, intra_ref, qdec_ref, kdec_ref, cdec_ref,
                     o_ref, state_ref):
    @pl.when(pl.program_id(1) == 0)
    def _():
        state_ref[...] = jnp.zeros_like(state_ref)

    c = c_ref[...]
    s = s_ref[...]
    heads = range(RET_HEADS)
    lanes = [slice(h * HEAD_DIM, (h + 1) * HEAD_DIM) for h in heads]

    def rope(t):
        return t * c + pltpu.roll(t, HEAD_DIM // 2, 1) * s

    q = [rope(q_ref[0, :, lanes[h]]) for h in heads]
    k = [rope(k_ref[0, :, lanes[h]]) * HEAD_DIM ** -0.5 for h in heads]
    v = [v_ref[0, :, lanes[h]] for h in heads]
    st = [state_ref[h] for h in heads]
    scores = [_dot(q[h], k[h], _NT) * intra_ref[h] for h in heads]
    cross = [_dot(q[h] * qdec_ref[h], st[h]) for h in heads]
    inner = [_dot(scores[h], v[h]) for h in heads]
    kv = [_dot(k[h] * kdec_ref[h], v[h], _TN) for h in heads]
    for h in heads:
        o_ref[0, :, lanes[h]] = inner[h] + cross[h]
        state_ref[h] = st[h] * cdec_ref[h] + kv[h]


def retention_scan(p_ret, tab_c, tab_s, n_lat, reverse):
    b, n, _ = p_ret.shape
    nb, nlb = n // RET_CHUNK, n_lat // RET_CHUNK
    consts = _ret_consts(reverse)

    def blk(s):
        return _scan_block(s, nb, nlb, reverse)

    def col(j):
        return pl.BlockSpec((1, RET_CHUNK, RET_W), lambda bi, s: (bi, blk(s), j))

    tab = pl.BlockSpec((RET_CHUNK, HEAD_DIM), lambda bi, s: (blk(s), 0))
    return pl.pallas_call(
        _ret_scan_kernel,
        grid=(b, nb),
        in_specs=[col(0), col(1), col(2), tab, tab]
        + [pl.BlockSpec(a.shape, lambda bi, s: (0, 0, 0)) for a in consts],
        out_specs=pl.BlockSpec((1, RET_CHUNK, RET_W), lambda bi, s: (bi, blk(s), 0)),
        out_shape=jax.ShapeDtypeStruct((b, n, RET_W), F32),
        scratch_shapes=[pltpu.VMEM((RET_HEADS, HEAD_DIM, HEAD_DIM), F32)],
        compiler_params=pltpu.CompilerParams(
            dimension_semantics=("parallel", "arbitrary"), vmem_limit_bytes=VMEM_LIMIT),
    )(p_ret, p_ret, p_ret, tab_c, tab_s, *consts)


def _silu(x):
    return x * jax.nn.sigmoid(x)


def _ret_final_kernel(of_ref, ob_ref, g_ref, w_ref, o_ref):
    for h in range(RET_HEADS):
        lane = slice(h * HEAD_DIM, (h + 1) * HEAD_DIM)
        o = of_ref[:, lane] + ob_ref[:, lane]
        mu = jnp.mean(o, axis=1, keepdims=True)
        d = o - mu
        var = jnp.mean(d * d, axis=1, keepdims=True)
        y = d * lax.rsqrt(var + NORM_EPS) * w_ref[:, lane] * _silu(g_ref[:, lane])
        o_ref[:, lane] = y.astype(o_ref.dtype)


def _gdn_final_kernel(of_ref, ob_ref, g_ref, w_ref, o_ref):
    for h in range(GDN_HEADS):
        lane = slice(h * HEAD_DIM, (h + 1) * HEAD_DIM)
        o = of_ref[:, lane] + ob_ref[:, lane]
        ms = jnp.mean(o * o, axis=1, keepdims=True)
        y = o * lax.rsqrt(ms + NORM_EPS) * w_ref[...] * _silu(g_ref[:, lane])
        o_ref[:, lane] = y.astype(o_ref.dtype)


FINAL_TT = 256


def mixer_finalize(kernel_fn, o_f, o_b, proj, gate_col, w):
    t, wd = o_f.shape
    row = pl.BlockSpec((FINAL_TT, wd), lambda i: (i, 0))
    return pl.pallas_call(
        kernel_fn,
        grid=(t // FINAL_TT,),
        in_specs=[row, row, pl.BlockSpec((FINAL_TT, wd), lambda i: (i, gate_col)),
                  pl.BlockSpec(w.shape, lambda i: (0, 0))],
        out_specs=row,
        out_shape=jax.ShapeDtypeStruct((t, wd), BF16),
        compiler_params=pltpu.CompilerParams(
            dimension_semantics=("parallel",), vmem_limit_bytes=VMEM_LIMIT),
    )(o_f, o_b, proj, w)


def ret_tables(cos, sin, n_ctx):
    c = jnp.concatenate([cos, cos], axis=1)
    s = jnp.concatenate([-sin, sin], axis=1)
    return (jnp.concatenate([c, jnp.ones((n_ctx, HEAD_DIM), F32)], axis=0),
            jnp.concatenate([s, jnp.zeros((n_ctx, HEAD_DIM), F32)], axis=0))


def retention_mixer(p_ret, norm_w, tabs, n_lat):
    b, n, _ = p_ret.shape
    o_f = retention_scan(p_ret, *tabs, n_lat, False)
    o_b = retention_scan(p_ret, *tabs, n_lat, True)
    out = mixer_finalize(_ret_final_kernel, o_f.reshape(b * n, RET_W), o_b.reshape(b * n, RET_W),
                         p_ret.reshape(b * n, 4 * RET_W), 3, norm_w[None, :])
    return out.reshape(b, n, RET_W)


GDN_TT = 256
CONV_HALO = SUBLANES


def _gdn_conv_kernel(main_ref, prev_ref, next_ref, w_ref, o_ref, ext_ref, *, tiles, lat_tiles):
    i = pl.program_id(0) % tiles
    sec = pl.program_id(1)
    seg_first = jnp.logical_or(i == 0, i == lat_tiles)
    seg_last = jnp.logical_or(i == lat_tiles - 1, i == tiles - 1)
    ext_ref[0:CONV_HALO, :] = jnp.where(seg_first, 0.0, prev_ref[...])
    ext_ref[CONV_HALO:CONV_HALO + GDN_TT, :] = main_ref[...]
    ext_ref[CONV_HALO + GDN_TT:, :] = jnp.where(seg_last, 0.0, next_ref[...])
    acc = jnp.zeros((GDN_TT, GDN_W), F32)
    for j in range(CONV_W):
        lo = CONV_HALO - CONV_W // 2 + j
        acc = acc + ext_ref[lo:lo + GDN_TT, :] * w_ref[j:j + 1, :]
    y = _silu(acc)
    scale = jnp.where(sec == 0, HEAD_DIM ** -0.5, 1.0)
    for h in range(GDN_HEADS):
        lane = slice(h * HEAD_DIM, (h + 1) * HEAD_DIM)
        yh = y[:, lane]
        nh = yh * (lax.rsqrt(jnp.sum(yh * yh, axis=1, keepdims=True) + NORM_EPS) * scale)
        o_ref[0, :, lane] = jnp.where(sec == 2, yh, nh)


def gdn_conv(p_gdn, conv_w, n_tot, n_lat):
    t = p_gdn.shape[0]
    tiles, lat_tiles = n_tot // GDN_TT, n_lat // GDN_TT
    per = GDN_TT // CONV_HALO
    last = t // CONV_HALO - 1
    return pl.pallas_call(
        functools.partial(_gdn_conv_kernel, tiles=tiles, lat_tiles=lat_tiles),
        grid=(t // GDN_TT, 3),
        in_specs=[pl.BlockSpec((GDN_TT, GDN_W), lambda i, s: (i, s)),
                  pl.BlockSpec((CONV_HALO, GDN_W), lambda i, s: (jnp.maximum(i * per - 1, 0), s)),
                  pl.BlockSpec((CONV_HALO, GDN_W), lambda i, s: (jnp.minimum((i + 1) * per, last), s)),
                  pl.BlockSpec((CONV_W, GDN_W), lambda i, s: (0, s))],
        out_specs=pl.BlockSpec((1, GDN_TT, GDN_W), lambda i, s: (s, i, 0)),
        out_shape=jax.ShapeDtypeStruct((3, t, GDN_W), F32),
        scratch_shapes=[pltpu.VMEM((GDN_TT + 2 * CONV_HALO, GDN_W), F32)],
        compiler_params=pltpu.CompilerParams(
            dimension_semantics=("parallel", "parallel"), vmem_limit_bytes=VMEM_LIMIT),
    )(p_gdn, p_gdn, p_gdn, conv_w)


def _gdn_gate_kernel(a_ref, alog_ref, dt_ref, sel_ref, gf_ref, gb_ref, bf_ref, bb_ref):
    x = a_ref[...]
    lane = lax.broadcasted_iota(jnp.int32, x.shape, 1)
    g = -jnp.exp(alog_ref[...]) * jax.nn.softplus(x + dt_ref[...])
    gate = jnp.where(lane < 2 * GDN_HEADS, g, jax.nn.sigmoid(x))
    for idx, ref in enumerate((gf_ref, gb_ref, bf_ref, bb_ref)):
        ref[...] = _dot(gate, sel_ref[idx], precision=lax.Precision.HIGHEST)


def gdn_gates(p_sml, a_log, dt_bias):
    t = p_sml.shape[0]
    pad = LANES - 2 * GDN_HEADS
    alog = jnp.pad(a_log.reshape(1, 2 * GDN_HEADS).astype(F32), ((0, 0), (0, pad)))
    dt = jnp.pad(dt_bias.reshape(1, 2 * GDN_HEADS).astype(F32), ((0, 0), (0, pad)))
    sel = np.zeros((4, LANES, GDN_W), np.float32)
    for idx in range(4):
        for h in range(GDN_HEADS):
            sel[idx, idx * GDN_HEADS + h, h * HEAD_DIM:(h + 1) * HEAD_DIM] = 1.0
    wide = jax.ShapeDtypeStruct((t, GDN_W), F32)
    wide_spec = pl.BlockSpec((GDN_TT, GDN_W), lambda i: (i, 0))
    vec_spec = pl.BlockSpec((1, LANES), lambda i: (0, 0))
    return pl.pallas_call(
        _gdn_gate_kernel,
        grid=(t // GDN_TT,),
        in_specs=[pl.BlockSpec((GDN_TT, LANES), lambda i: (i, 0)), vec_spec, vec_spec,
                  pl.BlockSpec(sel.shape, lambda i: (0, 0, 0))],
        out_specs=[wide_spec] * 4,
        out_shape=[wide] * 4,
        compiler_params=pltpu.CompilerParams(
            dimension_semantics=("parallel",), vmem_limit_bytes=VMEM_LIMIT),
    )(p_sml, alog, dt, jnp.asarray(sel))


def _gdn_scan_kernel(q_ref, k_ref, v_ref, g_ref, b_ref, o_ref, state_ref, *, reverse):
    @pl.when(pl.program_id(1) == 0)
    def _():
        state_ref[...] = jnp.zeros_like(state_ref)

    cl = GDN_CHUNK
    hi = lax.Precision.HIGHEST
    r = lax.broadcasted_iota(jnp.int32, (cl, cl), 0)
    c = lax.broadcasted_iota(jnp.int32, (cl, cl), 1)
    inc = (r <= c) if reverse else (r >= c)
    inc_f = inc.astype(F32)
    inc_t = ((c <= r) if reverse else (c >= r)).astype(F32)
    eye = (r == c).astype(F32)
    strict_f = inc_f - eye
    ones = jnp.ones((cl, HEAD_DIM), F32)
    n_chunks = SCAN_ROWS // cl
    order = range(n_chunks - 1, -1, -1) if reverse else range(n_chunks)
    heads = range(GDN_HEADS)
    items = [(h, ci) for ci in order for h in heads]

    def load(ref, lead):
        return [ref[lead + (slice(ci * cl, (ci + 1) * cl), slice(h * HEAD_DIM, (h + 1) * HEAD_DIM))]
                for h, ci in items]

    q, k, v = load(q_ref, (0, 0)), load(k_ref, (0, 0)), load(v_ref, (0, 0))
    gc, bc = load(g_ref, (0,)), load(b_ref, (0,))
    n = range(len(items))
    gcum = [_dot(inc_f, gc[i], precision=hi) for i in n]
    g_rows = [_dot(ones, gc[i], _NT, precision=hi) * (1.0 / HEAD_DIM) for i in n]
    gcum_row = [_dot(g_rows[i], inc_t, precision=hi) for i in n]
    gtot = [jnp.sum(gc[i], axis=0, keepdims=True) for i in n]
    decay = [jnp.where(inc, jnp.exp(jnp.where(inc, gcum[i][:, :cl] - gcum_row[i], 0.0)), 0.0) for i in n]
    eg = [jnp.exp(gcum[i]) for i in n]
    kb = [k[i] * bc[i] for i in n]
    a_pow = [_dot(kb[i], k[i], _NT) * decay[i] * strict_f for i in n]
    t_mat = [eye - a_pow[i] for i in n]
    for _ in range(int(math.log2(cl)) - 1):
        a_pow = [_dot(a_pow[i], a_pow[i]) for i in n]
        t_mat = [t_mat[i] + _dot(t_mat[i], a_pow[i]) for i in n]
    u = [_dot(t_mat[i], v[i] * bc[i]) for i in n]
    w = [_dot(t_mat[i], kb[i] * eg[i]) for i in n]
    attn = [_dot(q[i], k[i], _NT) * decay[i] for i in n]
    qd = [q[i] * eg[i] for i in n]
    kd = [k[i] * jnp.exp(gtot[i] - gcum[i]) for i in n]
    cd = [jnp.exp(gtot[i]) for i in n]
    st = [state_ref[h] for h in heads]
    for pos, ci in enumerate(order):
        idx = [pos * GDN_HEADS + h for h in heads]
        v_new = [u[i] - _dot(w[i], st[h]) for h, i in zip(heads, idx)]
        out = [_dot(qd[i], st[h]) + _dot(attn[i], v_new[h]) for h, i in zip(heads, idx)]
        st = [st[h] * cd[i] + _dot(kd[i], v_new[h], _TN) for h, i in zip(heads, idx)]
        for h in heads:
            o_ref[0, ci * cl:(ci + 1) * cl, h * HEAD_DIM:(h + 1) * HEAD_DIM] = out[h]
    for h in heads:
        state_ref[h] = st[h]


def gdn_scan(qkv, g, beta, n_lat, reverse):
    _, b, n, _ = qkv.shape
    nb, nlb = n // SCAN_ROWS, n_lat // SCAN_ROWS

    def blk(s):
        return _scan_block(s, nb, nlb, reverse)

    def sec(j):
        return pl.BlockSpec((1, 1, SCAN_ROWS, GDN_W), lambda bi, s: (j, bi, blk(s), 0))

    row = pl.BlockSpec((1, SCAN_ROWS, GDN_W), lambda bi, s: (bi, blk(s), 0))
    return pl.pallas_call(
        functools.partial(_gdn_scan_kernel, reverse=reverse),
        grid=(b, nb),
        in_specs=[sec(0), sec(1), sec(2), row, row],
        out_specs=row,
        out_shape=jax.ShapeDtypeStruct((b, n, GDN_W), F32),
        scratch_shapes=[pltpu.VMEM((GDN_HEADS, HEAD_DIM, HEAD_DIM), F32)],
        compiler_params=pltpu.CompilerParams(
            dimension_semantics=("parallel", "arbitrary"), vmem_limit_bytes=VMEM_LIMIT),
    )(qkv, qkv, qkv, g, beta)


def gdn_mixer(p_gdn, p_sml, lp, n_lat):
    b, n, _ = p_gdn.shape
    p_gdn2 = p_gdn.reshape(b * n, 4 * GDN_W)
    qkv = gdn_conv(p_gdn2, lp['conv_w'], n, n_lat).reshape(3, b, n, GDN_W)
    g_f, g_b, be_f, be_b = (a.reshape(b, n, GDN_W) for a in
                            gdn_gates(p_sml.reshape(b * n, -1), lp['gdn_a_log'], lp['gdn_dt_bias']))
    o_f = gdn_scan(qkv, g_f, be_f, n_lat, False)
    o_b = gdn_scan(qkv, g_b, be_b, n_lat, True)
    out = mixer_finalize(_gdn_final_kernel, o_f.reshape(b * n, GDN_W), o_b.reshape(b * n, GDN_W),
                         p_gdn2, 3, lp['gdn_norm_w'][None, :])
    return out.reshape(b, n, GDN_W)


def rms_norm(x, w):
    xf = x.astype(F32)
    y = xf * lax.rsqrt(jnp.mean(xf * xf, axis=-1, keepdims=True) + NORM_EPS)
    return (y * w.astype(F32)).astype(x.dtype)


def split_cols(t, sizes):
    parts, start = [], 0
    for s in sizes:
        parts.append(t[..., start:start + s])
        start += s
    return parts


def axial_rope_tables(row, col, rot_dim):
    n_freq = rot_dim // 4
    inv_freq = ROPE_BASE ** (-jnp.arange(n_freq, dtype=F32) / n_freq)
    ang = jnp.concatenate([row[:, None] * inv_freq, col[:, None] * inv_freq], axis=-1)
    return jnp.cos(ang), jnp.sin(ang)


MLA_HP = 2 * LANES
MLA_TT = 256
MLA_TQ = 512
MLA_TK = 2048
MLA_Q_SCALE = MLA_QK ** -0.5 * math.log2(math.e)


def _rope_tail(tail, c, s1, s2):
    half = ROPE_DIM // 2
    return tail * c + pltpu.roll(tail, LANES - half, 1) * s1 + pltpu.roll(tail, half, 1) * s2


def _mla_q_prep_kernel(x_ref, w_ref, c_ref, s1_ref, s2_ref, o_ref):
    c, s1, s2 = c_ref[...], s1_ref[...], s2_ref[...]
    for h in range(MLA_HEADS):
        lo = h * MLA_HP
        x = x_ref[:, lo:lo + MLA_HP]
        ms = jnp.sum(x * x, axis=1, keepdims=True) * (1.0 / MLA_QK)
        y = x * lax.rsqrt(ms + NORM_EPS) * w_ref[...]
        tail = _rope_tail(y[:, NOPE_DIM:], c, s1, s2)
        o_ref[:, lo:lo + NOPE_DIM] = (y[:, :NOPE_DIM] * MLA_Q_SCALE).astype(o_ref.dtype)
        o_ref[:, lo + NOPE_DIM:lo + MLA_HP] = (tail * MLA_Q_SCALE).astype(o_ref.dtype)


def _mla_kv_prep_kernel(kv_ref, kr_ref, w_ref, c_ref, s1_ref, s2_ref, k_ref, v_ref):
    c, s1, s2 = c_ref[...], s1_ref[...], s2_ref[...]
    kr = kr_ref[...]
    kr_ss = jnp.sum(kr * kr, axis=1, keepdims=True)
    for h in range(MLA_HEADS):
        lo = h * MLA_HP
        nope = kv_ref[:, lo:lo + NOPE_DIM]
        ms = (jnp.sum(nope * nope, axis=1, keepdims=True) + kr_ss) * (1.0 / MLA_QK)
        r = lax.rsqrt(ms + NORM_EPS)
        k_ref[:, lo:lo + NOPE_DIM] = (nope * r * w_ref[:, :NOPE_DIM]).astype(k_ref.dtype)
        tail = _rope_tail(kr * r * w_ref[:, NOPE_DIM:], c, s1, s2)
        k_ref[:, lo + NOPE_DIM:lo + MLA_HP] = tail.astype(k_ref.dtype)
        v_ref[:, h * V_DIM:(h + 1) * V_DIM] = kv_ref[:, lo + NOPE_DIM:lo + NOPE_DIM + V_DIM].astype(v_ref.dtype)


def mla_prep(qraw, kvraw, kr_pad, qw_pad, kw_pad, tabs, n_tot):
    t = qraw.shape[0]
    tiles = n_tot // MLA_TT
    tab_spec = pl.BlockSpec((MLA_TT, LANES), lambda i: (i % tiles, 0))
    w_spec = pl.BlockSpec((1, MLA_HP), lambda i: (0, 0))
    wide_spec = pl.BlockSpec((MLA_TT, MLA_HEADS * MLA_HP), lambda i: (i, 0))
    params = pltpu.CompilerParams(dimension_semantics=("parallel",), vmem_limit_bytes=VMEM_LIMIT)
    q = pl.pallas_call(
        _mla_q_prep_kernel,
        grid=(t // MLA_TT,),
        in_specs=[wide_spec, w_spec, tab_spec, tab_spec, tab_spec],
        out_specs=wide_spec,
        out_shape=jax.ShapeDtypeStruct((t, MLA_HEADS * MLA_HP), BF16),
        compiler_params=params,
    )(qraw, qw_pad, *tabs)
    k, v = pl.pallas_call(
        _mla_kv_prep_kernel,
        grid=(t // MLA_TT,),
        in_specs=[wide_spec, pl.BlockSpec((MLA_TT, LANES), lambda i: (i, 0)), w_spec,
                  tab_spec, tab_spec, tab_spec],
        out_specs=[wide_spec, pl.BlockSpec((MLA_TT, MLA_HEADS * V_DIM), lambda i: (i, 0))],
        out_shape=[jax.ShapeDtypeStruct((t, MLA_HEADS * MLA_HP), BF16),
                   jax.ShapeDtypeStruct((t, MLA_HEADS * V_DIM), BF16)],
        compiler_params=params,
    )(kvraw, kr_pad, kw_pad, *tabs)
    return q, k, v


def _mla_flash_kernel(q_ref, k_ref, v_ref, o_ref, *, n_loop, tail_len):
    q = q_ref[0]
    tq = q.shape[0]

    def attend(kc, vc, carry):
        m, l, acc = carry
        s = lax.dot_general(q, kc, (((1,), (1,)), ((), ())), preferred_element_type=F32)
        m_new = jnp.maximum(m, jnp.max(s, axis=1, keepdims=True))
        alpha = jnp.exp2(m - m_new)
        p = jnp.exp2(s - m_new)
        l = alpha * l + jnp.sum(p, axis=1, keepdims=True)
        acc = alpha * acc + jnp.dot(p.astype(vc.dtype), vc, preferred_element_type=F32)
        return m_new, l, acc

    def body(c, carry):
        off = pl.multiple_of(c * MLA_TK, MLA_TK)
        return attend(k_ref[0, pl.ds(off, MLA_TK), :], v_ref[0, pl.ds(off, MLA_TK), :], carry)

    carry = (jnp.full((tq, 1), -jnp.inf, F32), jnp.zeros((tq, 1), F32), jnp.zeros((tq, V_DIM), F32))
    if n_loop:
        carry = lax.fori_loop(0, n_loop, body, carry)
    if tail_len:
        lo = n_loop * MLA_TK
        carry = attend(k_ref[0, lo:lo + tail_len, :], v_ref[0, lo:lo + tail_len, :], carry)
    _, l, acc = carry
    o_ref[0] = (acc / l).astype(o_ref.dtype)


def mla_attend(q, k, v, n_lat, n_ctx, context_queries):
    b = q.shape[0]
    params = pltpu.CompilerParams(dimension_semantics=("parallel", "parallel", "arbitrary"),
                                  vmem_limit_bytes=VMEM_LIMIT)
    if context_queries:
        blk = n_lat // n_ctx
        return pl.pallas_call(
            functools.partial(_mla_flash_kernel, n_loop=0, tail_len=n_ctx),
            grid=(b, MLA_HEADS, 1),
            in_specs=[pl.BlockSpec((1, n_ctx, MLA_HP), lambda bi, h, i: (bi, blk, h)),
                      pl.BlockSpec((1, n_ctx, MLA_HP), lambda bi, h, i: (bi, blk, h)),
                      pl.BlockSpec((1, n_ctx, V_DIM), lambda bi, h, i: (bi, blk, h))],
            out_specs=pl.BlockSpec((1, n_ctx, V_DIM), lambda bi, h, i: (bi, 0, h)),
            out_shape=jax.ShapeDtypeStruct((b, n_ctx, MLA_HEADS * V_DIM), BF16),
            compiler_params=params,
        )(q, k, v)
    n_tot = n_lat + n_ctx
    assert n_lat % MLA_TK == 0 and n_lat % MLA_TQ == 0
    return pl.pallas_call(
        functools.partial(_mla_flash_kernel, n_loop=n_lat // MLA_TK, tail_len=n_ctx),
        grid=(b, MLA_HEADS, n_lat // MLA_TQ),
        in_specs=[pl.BlockSpec((1, MLA_TQ, MLA_HP), lambda bi, h, i: (bi, i, h)),
                  pl.BlockSpec((1, n_tot, MLA_HP), lambda bi, h, i: (bi, 0, h)),
                  pl.BlockSpec((1, n_tot, V_DIM), lambda bi, h, i: (bi, 0, h))],
        out_specs=pl.BlockSpec((1, MLA_TQ, V_DIM), lambda bi, h, i: (bi, i, h)),
        out_shape=jax.ShapeDtypeStruct((b, n_lat, MLA_HEADS * V_DIM), BF16),
        compiler_params=params,
    )(q, k, v)


def mla_tables(cos, sin, n_ctx):
    n_lat, half = cos.shape
    z = jnp.zeros((n_lat, LANES - 2 * half), F32)
    zh = jnp.zeros((n_lat, half), F32)
    c = jnp.concatenate([cos, cos, z], axis=1)
    s1 = jnp.concatenate([-sin, zh, z], axis=1)
    s2 = jnp.concatenate([zh, sin, z], axis=1)
    ident = jnp.concatenate([jnp.ones((n_ctx, 2 * half), F32), jnp.zeros((n_ctx, LANES - 2 * half), F32)], axis=1)
    zero = jnp.zeros((n_ctx, LANES), F32)
    return (jnp.concatenate([c, ident], axis=0), jnp.concatenate([s1, zero], axis=0),
            jnp.concatenate([s2, zero], axis=0))


def mla_mixer(c_q, c_kv, k_rope, lp, tabs, n_lat, with_ctx):
    b, n, _ = c_q.shape
    n_ctx = n - n_lat
    qraw = matmul(rms_norm(c_q, lp['cq_norm_w']).reshape(b * n, Q_LORA).astype(BF16), lp['w_uq_pad'])
    kvraw = matmul(rms_norm(c_kv, lp['ckv_norm_w']).reshape(b * n, KV_LORA).astype(BF16), lp['w_ukv'])
    kr_pad = jnp.pad(k_rope.reshape(b * n, ROPE_DIM), ((0, 0), (0, LANES - ROPE_DIM)))
    q, k, v = mla_prep(qraw, kvraw, kr_pad, lp['q_norm_pad'], lp['k_norm_pad'], tabs, n)
    q = q.reshape(b, n, -1)
    k = k.reshape(b, n, -1)
    v = v.reshape(b, n, -1)
    o_x = mla_attend(q, k, v, n_lat, n_ctx, False)
    if not with_ctx:
        return o_x
    return jnp.concatenate([o_x, mla_attend(q, k, v, n_lat, n_ctx, True)], axis=1)


RN_TT = 256


def _resid_norm_kernel(*refs, has_resid, has_norm):
    refs = list(refs)
    x_ref = refs.pop(0)
    y_ref = refs.pop(0) if has_resid else None
    vec_ref = refs.pop(0)
    w_ref = refs.pop(0) if has_norm else None
    xo_ref = refs.pop(0) if has_resid else None
    h_ref = refs.pop(0) if has_norm else None
    x = x_ref[0]
    if has_resid:
        x = x + vec_ref[0, 0, 0:1, :] * y_ref[0].astype(F32)
        xo_ref[0] = x
    if has_norm:
        ms = jnp.mean(x * x, axis=1, keepdims=True)
        hn = x * lax.rsqrt(ms + NORM_EPS) * w_ref[...]
        h_ref[0] = (hn * (1.0 + vec_ref[0, 0, 2:3, :]) + vec_ref[0, 0, 1:2, :]).astype(h_ref.dtype)


def resid_norm(x, y, vec, norm_w, n_rows, n_lat):
    b, _, d = x.shape
    lat_tiles = n_lat // RN_TT
    row = pl.BlockSpec((1, RN_TT, d), lambda bi, i: (bi, i, 0))
    in_specs, args, out_specs, out_shape = [row], [x], [], []
    if y is not None:
        in_specs.append(row)
        args.append(y)
        out_specs.append(row)
        out_shape.append(jax.ShapeDtypeStruct((b, n_rows, d), F32))
    in_specs.append(pl.BlockSpec((1, 1, SUBLANES, d), lambda bi, i: (bi, (i >= lat_tiles).astype(jnp.int32), 0, 0)))
    args.append(vec)
    if norm_w is not None:
        in_specs.append(pl.BlockSpec((1, d), lambda bi, i: (0, 0)))
        args.append(norm_w[None, :])
        out_specs.append(row)
        out_shape.append(jax.ShapeDtypeStruct((b, n_rows, d), BF16))
    return pl.pallas_call(
        functools.partial(_resid_norm_kernel, has_resid=y is not None, has_norm=norm_w is not None),
        grid=(b, n_rows // RN_TT),
        in_specs=in_specs, out_specs=out_specs, out_shape=out_shape,
        compiler_params=pltpu.CompilerParams(
            dimension_semantics=("parallel", "parallel"), vmem_limit_bytes=VMEM_LIMIT),
    )(*args)


def mod_vec(gate_x, gate_c, shift_x, shift_c, scale_x, scale_c):
    lat = jnp.stack([gate_x, shift_x, scale_x], axis=1)
    con = jnp.broadcast_to(jnp.stack([gate_c, shift_c, scale_c], axis=0)[None], lat.shape)
    v = jnp.stack([lat, con], axis=1)
    return jnp.pad(v, ((0, 0), (0, 0), (0, SUBLANES - v.shape[2]), (0, 0)))


def trunk_layer(xs, pending, mod_x, mod_c, lp, ret_tabs, mla_tabs, n_lat, last):
    b, n_tot, d = xs.shape
    sh1, sc1, g1, sh2, sc2, g2 = jnp.split(mod_x, 6, axis=-1)
    csh1, csc1, cg1, csh2, csc2, cg2 = jnp.split(mod_c, 6, axis=-1)
    if pending is None:
        (h,) = resid_norm(xs, None, mod_vec(g1, cg1, sh1, csh1, sc1, csc1), lp['norm1_w'], n_tot, n_lat)
    else:
        y_prev, pg, pcg = pending
        xs, h = resid_norm(xs, y_prev, mod_vec(pg, pcg, sh1, csh1, sc1, csc1), lp['norm1_w'], n_tot, n_lat)
    h2 = h.reshape(b * n_tot, d)
    p_ret = matmul(h2, lp['w_in_ret']).reshape(b, n_tot, 4 * RET_W)
    p_gdn = matmul(h2, lp['w_in_gdn']).reshape(b, n_tot, 4 * GDN_W)
    p_sml = matmul(h2, lp['w_in_sml'], tn=2048).reshape(b, n_tot, -1)
    _, _, _, _, c_q, c_kv, k_rope = split_cols(p_sml, (GDN_HEADS,) * 4 + (Q_LORA, KV_LORA, ROPE_DIM))
    ret_o = retention_mixer(p_ret, lp['ret_norm_w'], ret_tabs, n_lat)
    gdn_o = gdn_mixer(p_gdn, p_sml, lp, n_lat)
    mla_o = mla_mixer(c_q, c_kv, k_rope, lp, mla_tabs, n_lat, not last)
    n_out = n_lat if last else n_tot
    mixed = jnp.concatenate([ret_o[:, :n_out], gdn_o[:, :n_out], mla_o], axis=-1)
    y = matmul(mixed.reshape(b * n_out, MIX_W), lp['w_out'], BF16).reshape(b, n_out, d)
    xs, f = resid_norm(xs, y, mod_vec(g1, cg1, sh2, csh2, sc2, csc2), lp['norm2_w'], n_out, n_lat)
    f = peer_ffn(f.reshape(b * n_out, d), lp['peer_wq'], lp['peer_keys'], lp['peer_u'], lp['peer_vt'])
    return xs, (f.reshape(b, n_out, d), g2, cg2)


def kernel(x, c, ctx, c_ctx, ada_w, ada_b, norm1_w, norm2_w, w_in, conv_w, gdn_a_log, gdn_dt_bias,
           gdn_norm_w, ret_norm_w, cq_norm_w, ckv_norm_w, w_uq, w_ukv, q_norm_w, k_norm_w, w_out,
           peer_wq, peer_keys, peer_u, peer_v):
    n_lat = x.shape[1]
    n_rows = n_lat // GRID_W
    row = jnp.repeat(jnp.arange(n_rows, dtype=F32), GRID_W, total_repeat_length=n_lat)
    col = (jnp.arange(n_lat) % GRID_W).astype(F32)
    ret_tabs = ret_tables(*axial_rope_tables(row, col, HEAD_DIM), ctx.shape[1])
    mla_tabs = mla_tables(*axial_rope_tables(row, col, ROPE_DIM), ctx.shape[1])
    silu_c = jax.nn.silu(c)
    silu_cc = jax.nn.silu(c_ctx)
    sml_w = 4 * GDN_HEADS + Q_LORA + KV_LORA + ROPE_DIM
    sml_pad = -sml_w % LANES
    xs = jnp.concatenate([x, ctx], axis=1)
    pending = None
    for i in range(DEPTH):
        mod_x = silu_c @ ada_w[i] + ada_b[i]
        mod_c = silu_cc @ ada_w[i] + ada_b[i]
        w_in_i = w_in[i].astype(BF16)
        lp = {
            'norm1_w': norm1_w[i], 'norm2_w': norm2_w[i], 'conv_w': conv_w[i],
            'w_in_ret': w_in_i[:, :4 * RET_W],
            'w_in_gdn': w_in_i[:, 4 * RET_W:4 * RET_W + 4 * GDN_W],
            'w_in_sml': jnp.pad(w_in_i[:, 4 * RET_W + 4 * GDN_W:], ((0, 0), (0, sml_pad))),
            'gdn_a_log': gdn_a_log[i], 'gdn_dt_bias': gdn_dt_bias[i], 'gdn_norm_w': gdn_norm_w[i],
            'ret_norm_w': ret_norm_w[i], 'cq_norm_w': cq_norm_w[i], 'ckv_norm_w': ckv_norm_w[i],
            'w_uq_pad': jnp.pad(w_uq[i].astype(BF16).reshape(Q_LORA, MLA_HEADS, MLA_QK),
                                ((0, 0), (0, 0), (0, MLA_HP - MLA_QK))).reshape(Q_LORA, MLA_HEADS * MLA_HP),
            'w_ukv': w_ukv[i].astype(BF16),
            'q_norm_pad': jnp.pad(q_norm_w[i], (0, MLA_HP - MLA_QK))[None, :],
            'k_norm_pad': jnp.pad(k_norm_w[i], (0, MLA_HP - MLA_QK))[None, :],
            'w_out': w_out[i].astype(BF16), 'peer_wq': peer_wq[i].astype(BF16), 'peer_keys': peer_keys[i],
            'peer_u': peer_u[i].astype(BF16), 'peer_vt': peer_v[i].astype(BF16).T,
        }
        xs, pending = trunk_layer(xs, pending, mod_x, mod_c, lp, ret_tabs, mla_tabs, n_lat, i == DEPTH - 1)
    f, g2, cg2 = pending
    (out,) = resid_norm(xs, f, mod_vec(g2, cg2, g2, cg2, g2, cg2), None, n_lat, n_lat)
    return out
```

```python
import functools
import math

import jax
import jax.numpy as jnp
import numpy as np
from jax import lax
from jax.experimental import pallas as pl
from jax.experimental.pallas import tpu as pltpu

D_MODEL = 4096
DEPTH = 2
GRID_W = 64
HEAD_DIM = 128
RET_HEADS = 8
RET_CHUNK = 128
GDN_HEADS = 8
GDN_CHUNK = 64
CONV_W = 5
MLA_HEADS = 16
Q_LORA = 1024
KV_LORA = 512
NOPE_DIM = 128
ROPE_DIM = 64
V_DIM = 128
MLA_QK = NOPE_DIM + ROPE_DIM
ATTN_BLOCK = 128
PEER_HEADS = 8
N_KEYS = 128
N_EXPERTS = N_KEYS * N_KEYS
PEER_TOPK = 16
PEER_QDIM = 256
ROPE_BASE = 10000.0
NORM_EPS = 1e-6
RET_W = RET_HEADS * HEAD_DIM
GDN_W = GDN_HEADS * HEAD_DIM
MLA_W = MLA_HEADS * V_DIM
MIX_W = RET_W + GDN_W + MLA_W
IN_SIZES = (RET_W,) * 4 + (GDN_W,) * 4 + (GDN_HEADS,) * 4 + (Q_LORA, KV_LORA, ROPE_DIM)

LANES = 128
SUBLANES = 8
VMEM_LIMIT = 56 * 1024 * 1024

F32 = jnp.float32
BF16 = jnp.bfloat16


def _mm_kernel(a_ref, b_ref, o_ref):
    o_ref[...] = jnp.dot(a_ref[...].astype(BF16), b_ref[...].astype(BF16),
                         preferred_element_type=F32).astype(o_ref.dtype)


def _pick_tile(n, target, align):
    best = n
    for t in range(align, min(n, target) + 1, align):
        if n % t == 0:
            best = t
    return best


def matmul(a, b, out_dtype=F32, tm=512, tn=512):
    m, k = a.shape
    n = b.shape[1]
    tm = _pick_tile(m, tm, 8)
    tn = _pick_tile(n, tn, LANES)
    return pl.pallas_call(
        _mm_kernel,
        grid=(m // tm, n // tn),
        in_specs=[pl.BlockSpec((tm, k), lambda i, j: (i, 0)),
                  pl.BlockSpec((k, tn), lambda i, j: (0, j))],
        out_specs=pl.BlockSpec((tm, tn), lambda i, j: (i, j)),
        out_shape=jax.ShapeDtypeStruct((m, n), out_dtype),
        compiler_params=pltpu.CompilerParams(
            dimension_semantics=("parallel", "parallel"), vmem_limit_bytes=VMEM_LIMIT),
    )(a, b)


PEER_TT = 512
PEER_EB = 512
PEER_JB = PEER_EB // N_KEYS
assert SUBLANES == 2 * PEER_JB


def _gelu(a):
    return 0.5 * a * (1.0 + lax.erf(a * (2.0 ** -0.5)))


def _peer_kernel(x_ref, u_ref, vt_ref, s1_ref, e1_ref, s2_ref, e2_ref, tau_ref, o_ref, acc_ref):
    e = pl.program_id(1)

    @pl.when(e == 0)
    def _():
        acc_ref[...] = jnp.zeros_like(acc_ref)

    act_t = lax.dot_general(u_ref[...], x_ref[...], (((1,), (1,)), ((), ())),
                            preferred_element_type=F32)
    upper = (e % 2) == 1

    def key_rows(ref, h, lane):
        blk = ref[h, :, lane]
        return jnp.where(upper, pltpu.roll(blk, PEER_JB, 0), blk)

    rows = []
    for jj in range(PEER_JB):
        cols = []
        for ts in range(PEER_TT // LANES):
            lane = slice(ts * LANES, (ts + 1) * LANES)
            g = jnp.zeros((N_KEYS, LANES), F32)
            for h in range(PEER_HEADS):
                s1row = key_rows(s1_ref, h, lane)[jj:jj + 1, :]
                e1row = key_rows(e1_ref, h, lane)[jj:jj + 1, :]
                t = s2_ref[h, :, lane] + s1row
                g = g + jnp.where(t >= tau_ref[h:h + 1, lane], e2_ref[h, :, lane] * e1row, 0.0)
            a = act_t[jj * N_KEYS:(jj + 1) * N_KEYS, lane]
            cols.append((_gelu(a) * g).astype(BF16))
        rows.append(jnp.concatenate(cols, axis=1))
    c_t = jnp.concatenate(rows, axis=0)
    acc_ref[...] += jnp.dot(vt_ref[...], c_t, preferred_element_type=F32)

    @pl.when(e == pl.num_programs(1) - 1)
    def _():
        o_ref[...] = acc_ref[...].T.astype(o_ref.dtype)


def peer_dense(x, u, vt, s1, e1, s2, e2, tau):
    t, d = x.shape
    n_e = u.shape[0] // PEER_EB
    row_spec = pl.BlockSpec((PEER_HEADS, SUBLANES, PEER_TT),
                            lambda i, e: (0, e // (SUBLANES // PEER_JB), i))
    return pl.pallas_call(
        _peer_kernel,
        grid=(t // PEER_TT, n_e),
        in_specs=[
            pl.BlockSpec((PEER_TT, d), lambda i, e: (i, 0)),
            pl.BlockSpec((PEER_EB, d), lambda i, e: (e, 0)),
            pl.BlockSpec((d, PEER_EB), lambda i, e: (0, e)),
            row_spec,
            row_spec,
            pl.BlockSpec((PEER_HEADS, N_KEYS, PEER_TT), lambda i, e: (0, 0, i)),
            pl.BlockSpec((PEER_HEADS, N_KEYS, PEER_TT), lambda i, e: (0, 0, i)),
            pl.BlockSpec((PEER_HEADS, PEER_TT), lambda i, e: (0, i)),
        ],
        out_specs=pl.BlockSpec((PEER_TT, d), lambda i, e: (i, 0)),
        out_shape=jax.ShapeDtypeStruct((t, d), BF16),
        scratch_shapes=[pltpu.VMEM((d, PEER_TT), F32)],
        compiler_params=pltpu.CompilerParams(
            dimension_semantics=("parallel", "arbitrary"), vmem_limit_bytes=VMEM_LIMIT),
    )(x, u, vt, s1, e1, s2, e2, tau)


def _extract_top(work_ref, top_ref, n_groups):
    n_rows = work_ref.shape[1]
    iota = lax.broadcasted_iota(jnp.int32, (n_rows, LANES), 0)

    def body(r, carry):
        for g in range(n_groups):
            w = work_ref[g]
            m = jnp.max(w, axis=0, keepdims=True)
            top_ref[g, pl.ds(r, 1), :] = m
            first = jnp.min(jnp.where(w == m, iota, n_rows), axis=0, keepdims=True)
            work_ref[g] = jnp.where(iota == first, -jnp.inf, w)
        return carry

    lax.fori_loop(0, PEER_TOPK, body, 0)


def _peer_select_kernel(q_ref, keys_ref, s1_ref, e1_ref, s2_ref, e2_ref, tau_ref,
                        work_ref, top_ref, cand_ref, cwork_ref, ctop_ref):
    for h in range(PEER_HEADS):
        for p in range(2):
            g = 2 * h + p
            sc = lax.dot_general(keys_ref[h, p], q_ref[:, g * N_KEYS:(g + 1) * N_KEYS],
                                 (((1,), (1,)), ((), ())), preferred_element_type=F32,
                                 precision=lax.Precision.HIGHEST)
            work_ref[g] = sc
            (s1_ref, s2_ref)[p][h] = sc
    _extract_top(work_ref, top_ref, 2 * PEER_HEADS)
    half = PEER_TOPK // 2
    for h in range(PEER_HEADS):
        t1 = top_ref[2 * h]
        t2 = top_ref[2 * h + 1]
        pieces = [t1[0:1, :] + t2]
        pieces += [t1[p:p + 1, :] + t2[:half, :] for p in range(1, half)]
        pieces += [t1[half:, :] + t2[0:1, :]]
        lo = 0
        for c in pieces:
            cand_ref[h, lo:lo + c.shape[0], :] = c
            cwork_ref[h, lo:lo + c.shape[0], :] = c
            lo += c.shape[0]
    _extract_top(cwork_ref, ctop_ref, PEER_HEADS)
    for h in range(PEER_HEADS):
        tau = ctop_ref[h, PEER_TOPK - 1:PEER_TOPK, :]
        m1 = top_ref[2 * h, 0:1, :]
        m2 = top_ref[2 * h + 1, 0:1, :]
        cand = cand_ref[h]
        z = jnp.sum(jnp.where(cand >= tau, jnp.exp(cand - (m1 + m2)), 0.0), axis=0, keepdims=True)
        tau_ref[h:h + 1, :] = tau
        e1_ref[h] = jnp.exp(s1_ref[h] - m1)
        e2_ref[h] = jnp.exp(s2_ref[h] - m2) / z


def peer_select(q, keys):
    t = q.shape[0]
    big = jax.ShapeDtypeStruct((PEER_HEADS, N_KEYS, t), F32)
    big_spec = pl.BlockSpec((PEER_HEADS, N_KEYS, LANES), lambda i: (0, 0, i))
    n_cand = PEER_TOPK + (PEER_TOPK // 2 - 1) * (PEER_TOPK // 2) + PEER_TOPK // 2
    return pl.pallas_call(
        _peer_select_kernel,
        grid=(t // LANES,),
        in_specs=[pl.BlockSpec((LANES, q.shape[1]), lambda i: (i, 0)),
                  pl.BlockSpec(keys.shape, lambda i: (0, 0, 0, 0))],
        out_specs=[big_spec, big_spec, big_spec, big_spec,
                   pl.BlockSpec((PEER_HEADS, LANES), lambda i: (0, i))],
        out_shape=[big, big, big, big, jax.ShapeDtypeStruct((PEER_HEADS, t), F32)],
        scratch_shapes=[pltpu.VMEM((2 * PEER_HEADS, N_KEYS, LANES), F32),
                        pltpu.VMEM((2 * PEER_HEADS, PEER_TOPK, LANES), F32),
                        pltpu.VMEM((PEER_HEADS, n_cand, LANES), F32),
                        pltpu.VMEM((PEER_HEADS, n_cand, LANES), F32),
                        pltpu.VMEM((PEER_HEADS, PEER_TOPK, LANES), F32)],
        compiler_params=pltpu.CompilerParams(
            dimension_semantics=("parallel",), vmem_limit_bytes=VMEM_LIMIT),
    )(q, keys)


def peer_ffn(f_bf, wq, sub_keys, u_bf, vt_bf):
    q = matmul(f_bf, wq, F32)
    s1, e1, s2, e2, tau = peer_select(q, sub_keys)
    return peer_dense(f_bf, u_bf, vt_bf, s1, e1, s2, e2, tau)


SCAN_ROWS = 128


def _scan_block(step, n_blocks, n_lat_blocks, reverse):
    return (n_blocks - 1 - step) if reverse else (step + n_lat_blocks) % n_blocks


def _dot(a, b, dims=(((1,), (0,)), ((), ())), precision=None):
    if precision is None:
        a, b = a.astype(BF16), b.astype(BF16)
    return lax.dot_general(a, b, dims, preferred_element_type=F32, precision=precision)


_NT = (((1,), (1,)), ((), ()))
_TN = (((0,), (0,)), ((), ()))


RET_LOG_GAMMA = [math.log1p(-(2.0 ** (-5.0 - h))) for h in range(RET_HEADS)]


def _ret_consts(reverse):
    lg = np.array(RET_LOG_GAMMA[::-1] if reverse else RET_LOG_GAMMA, np.float64)[:, None, None]
    pos = np.arange(RET_CHUNK, dtype=np.float64)
    diff = pos[:, None] - pos[None, :]
    rank = pos
    if reverse:
        diff, rank = -diff, RET_CHUNK - 1.0 - pos
    ones = np.ones((1, 1, HEAD_DIM))
    intra = np.where(diff >= 0, np.exp(lg * np.maximum(diff, 0.0)), 0.0)
    qdec = np.exp(lg * (rank + 1.0)[None, :, None]) * ones
    kdec = np.exp(lg * (RET_CHUNK - 1.0 - rank)[None, :, None]) * ones
    cdec = np.exp(lg * RET_CHUNK) * ones
    return [jnp.asarray(a, F32) for a in (intra, qdec, kdec, cdec)]


def _ret_scan_kernel(q_ref, k_ref, v_ref, c_ref, s_ref, intra_ref, qdec_ref, kdec_ref, cdec_ref,
                     o_ref, state_ref):
    @pl.when(pl.program_id(1) == 0)
    def _():
        state_ref[...] = jnp.zeros_like(state_ref)

    c = c_ref[...]
    s = s_ref[...]
    heads = range(RET_HEADS)
    lanes = [slice(h * HEAD_DIM, (h + 1) * HEAD_DIM) for h in heads]

    def rope(t):
        return t * c + pltpu.roll(t, HEAD_DIM // 2, 1) * s

    q = [rope(q_ref[0, :, lanes[h]]) for h in heads]
    k = [rope(k_ref[0, :, lanes[h]]) * HEAD_DIM ** -0.5 for h in heads]
    v = [v_ref[0, :, lanes[h]] for h in heads]
    st = [state_ref[h] for h in heads]
    scores = [_dot(q[h], k[h], _NT) * intra_ref[h] for h in heads]
    cross = [_dot(q[h] * qdec_ref[h], st[h]) for h in heads]
    inner = [_dot(scores[h], v[h]) for h in heads]
    kv = [_dot(k[h] * kdec_ref[h], v[h], _TN) for h in heads]
    for h in heads:
        o_ref[0, :, lanes[h]] = inner[h] + cross[h]
        state_ref[h] = st[h] * cdec_ref[h] + kv[h]


def retention_scan(p_ret, tab_c, tab_s, n_lat, reverse):
    b, n, _ = p_ret.shape
    nb, nlb = n // RET_CHUNK, n_lat // RET_CHUNK
    consts = _ret_consts(reverse)

    def blk(s):
        return _scan_block(s, nb, nlb, reverse)

    def col(j):
        return pl.BlockSpec((1, RET_CHUNK, RET_W), lambda bi, s: (bi, blk(s), j))

    tab = pl.BlockSpec((RET_CHUNK, HEAD_DIM), lambda bi, s: (blk(s), 0))
    return pl.pallas_call(
        _ret_scan_kernel,
        grid=(b, nb),
        in_specs=[col(0), col(1), col(2), tab, tab]
        + [pl.BlockSpec(a.shape, lambda bi, s: (0, 0, 0)) for a in consts],
        out_specs=pl.BlockSpec((1, RET_CHUNK, RET_W), lambda bi, s: (bi, blk(s), 0)),
        out_shape=jax.ShapeDtypeStruct((b, n, RET_W), F32),
        scratch_shapes=[pltpu.VMEM((RET_HEADS, HEAD_DIM, HEAD_DIM), F32)],
        compiler_params=pltpu.CompilerParams(
            dimension_semantics=("parallel", "arbitrary"), vmem_limit_bytes=VMEM_LIMIT),
    )(p_ret, p_ret, p_ret, tab_c, tab_s, *consts)


def _silu(x):
    return x * jax.nn.sigmoid(x)


def _ret_final_kernel(of_ref, ob_ref, g_ref, w_ref, o_ref):
    for h in range(RET_HEADS):
        lane = slice(h * HEAD_DIM, (h + 1) * HEAD_DIM)
        o = of_ref[:, lane] + ob_ref[:, lane]
        mu = jnp.mean(o, axis=1, keepdims=True)
        d = o - mu
        var = jnp.mean(d * d, axis=1, keepdims=True)
        y = d * lax.rsqrt(var + NORM_EPS) * w_ref[:, lane] * _silu(g_ref[:, lane])
        o_ref[:, lane] = y.astype(o_ref.dtype)


def _gdn_final_kernel(of_ref, ob_ref, g_ref, w_ref, o_ref):
    for h in range(GDN_HEADS):
        lane = slice(h * HEAD_DIM, (h + 1) * HEAD_DIM)
        o = of_ref[:, lane] + ob_ref[:, lane]
        ms = jnp.mean(o * o, axis=1, keepdims=True)
        y = o * lax.rsqrt(ms + NORM_EPS) * w_ref[...] * _silu(g_ref[:, lane])
        o_ref[:, lane] = y.astype(o_ref.dtype)


FINAL_TT = 256


def mixer_finalize(kernel_fn, o_f, o_b, proj, gate_col, w):
    t, wd = o_f.shape
    row = pl.BlockSpec((FINAL_TT, wd), lambda i: (i, 0))
    return pl.pallas_call(
        kernel_fn,
        grid=(t // FINAL_TT,),
        in_specs=[row, row, pl.BlockSpec((FINAL_TT, wd), lambda i: (i, gate_col)),
                  pl.BlockSpec(w.shape, lambda i: (0, 0))],
        out_specs=row,
        out_shape=jax.ShapeDtypeStruct((t, wd), BF16),
        compiler_params=pltpu.CompilerParams(
            dimension_semantics=("parallel",), vmem_limit_bytes=VMEM_LIMIT),
    )(o_f, o_b, proj, w)


def ret_tables(cos, sin, n_ctx):
    c = jnp.concatenate([cos, cos], axis=1)
    s = jnp.concatenate([-sin, sin], axis=1)
    return (jnp.concatenate([c, jnp.ones((n_ctx, HEAD_DIM), F32)], axis=0),
            jnp.concatenate([s, jnp.zeros((n_ctx, HEAD_DIM), F32)], axis=0))


def retention_mixer(p_ret, norm_w, tabs, n_lat):
    b, n, _ = p_ret.shape
    o_f = retention_scan(p_ret, *tabs, n_lat, False)
    o_b = retention_scan(p_ret, *tabs, n_lat, True)
    out = mixer_finalize(_ret_final_kernel, o_f.reshape(b * n, RET_W), o_b.reshape(b * n, RET_W),
                         p_ret.reshape(b * n, 4 * RET_W), 3, norm_w[None, :])
    return out.reshape(b, n, RET_W)


GDN_TT = 256
CONV_HALO = SUBLANES


def _gdn_conv_kernel(main_ref, prev_ref, next_ref, w_ref, o_ref, ext_ref, *, tiles, lat_tiles):
    i = pl.program_id(0) % tiles
    sec = pl.program_id(1)
    seg_first = jnp.logical_or(i == 0, i == lat_tiles)
    seg_last = jnp.logical_or(i == lat_tiles - 1, i == tiles - 1)
    ext_ref[0:CONV_HALO, :] = jnp.where(seg_first, 0.0, prev_ref[...])
    ext_ref[CONV_HALO:CONV_HALO + GDN_TT, :] = main_ref[...]
    ext_ref[CONV_HALO + GDN_TT:, :] = jnp.where(seg_last, 0.0, next_ref[...])
    acc = jnp.zeros((GDN_TT, GDN_W), F32)
    for j in range(CONV_W):
        lo = CONV_HALO - CONV_W // 2 + j
        acc = acc + ext_ref[lo:lo + GDN_TT, :] * w_ref[j:j + 1, :]
    y = _silu(acc)
    scale = jnp.where(sec == 0, HEAD_DIM ** -0.5, 1.0)
    for h in range(GDN_HEADS):
        lane = slice(h * HEAD_DIM, (h + 1) * HEAD_DIM)
        yh = y[:, lane]
        nh = yh * (lax.rsqrt(jnp.sum(yh * yh, axis=1, keepdims=True) + NORM_EPS) * scale)
        o_ref[0, :, lane] = jnp.where(sec == 2, yh, nh)


def gdn_conv(p_gdn, conv_w, n_tot, n_lat):
    t = p_gdn.shape[0]
    tiles, lat_tiles = n_tot // GDN_TT, n_lat // GDN_TT
    per = GDN_TT // CONV_HALO
    last = t // CONV_HALO - 1
    return pl.pallas_call(
        functools.partial(_gdn_conv_kernel, tiles=tiles, lat_tiles=lat_tiles),
        grid=(t // GDN_TT, 3),
        in_specs=[pl.BlockSpec((GDN_TT, GDN_W), lambda i, s: (i, s)),
                  pl.BlockSpec((CONV_HALO, GDN_W), lambda i, s: (jnp.maximum(i * per - 1, 0), s)),
                  pl.BlockSpec((CONV_HALO, GDN_W), lambda i, s: (jnp.minimum((i + 1) * per, last), s)),
                  pl.BlockSpec((CONV_W, GDN_W), lambda i, s: (0, s))],
        out_specs=pl.BlockSpec((1, GDN_TT, GDN_W), lambda i, s: (s, i, 0)),
        out_shape=jax.ShapeDtypeStruct((3, t, GDN_W), F32),
        scratch_shapes=[pltpu.VMEM((GDN_TT + 2 * CONV_HALO, GDN_W), F32)],
        compiler_params=pltpu.CompilerParams(
            dimension_semantics=("parallel", "parallel"), vmem_limit_bytes=VMEM_LIMIT),
    )(p_gdn, p_gdn, p_gdn, conv_w)


def _gdn_gate_kernel(a_ref, alog_ref, dt_ref, sel_ref, gf_ref, gb_ref, bf_ref, bb_ref):
    x = a_ref[...]
    lane = lax.broadcasted_iota(jnp.int32, x.shape, 1)
    g = -jnp.exp(alog_ref[...]) * jax.nn.softplus(x + dt_ref[...])
    gate = jnp.where(lane < 2 * GDN_HEADS, g, jax.nn.sigmoid(x))
    for idx, ref in enumerate((gf_ref, gb_ref, bf_ref, bb_ref)):
        ref[...] = _dot(gate, sel_ref[idx], precision=lax.Precision.HIGHEST)


def gdn_gates(p_sml, a_log, dt_bias):
    t = p_sml.shape[0]
    pad = LANES - 2 * GDN_HEADS
    alog = jnp.pad(a_log.reshape(1, 2 * GDN_HEADS).astype(F32), ((0, 0), (0, pad)))
    dt = jnp.pad(dt_bias.reshape(1, 2 * GDN_HEADS).astype(F32), ((0, 0), (0, pad)))
    sel = np.zeros((4, LANES, GDN_W), np.float32)
    for idx in range(4):
        for h in range(GDN_HEADS):
            sel[idx, idx * GDN_HEADS + h, h * HEAD_DIM:(h + 1) * HEAD_DIM] = 1.0
    wide = jax.ShapeDtypeStruct((t, GDN_W), F32)
    wide_spec = pl.BlockSpec((GDN_TT, GDN_W), lambda i: (i, 0))
    vec_spec = pl.BlockSpec((1, LANES), lambda i: (0, 0))
    return pl.pallas_call(
        _gdn_gate_kernel,
        grid=(t // GDN_TT,),
        in_specs=[pl.BlockSpec((GDN_TT, LANES), lambda i: (i, 0)), vec_spec, vec_spec,
                  pl.BlockSpec(sel.shape, lambda i: (0, 0, 0))],
        out_specs=[wide_spec] * 4,
        out_shape=[wide] * 4,
        compiler_params=pltpu.CompilerParams(
            dimension_semantics=("parallel",), vmem_limit_bytes=VMEM_LIMIT),
    )(p_sml, alog, dt, jnp.asarray(sel))


def _gdn_scan_kernel(q_ref, k_ref, v_ref, g_ref, b_ref, o_ref, state_ref, *, reverse):
    @pl.when(pl.program_id(1) == 0)
    def _():
        state_ref[...] = jnp.zeros_like(state_ref)

    cl = GDN_CHUNK
    hi = lax.Precision.HIGHEST
    r = lax.broadcasted_iota(jnp.int32, (cl, cl), 0)
    c = lax.broadcasted_iota(jnp.int32, (cl, cl), 1)
    inc = (r <= c) if reverse else (r >= c)
    inc_f = inc.astype(F32)
    inc_t = ((c <= r) if reverse else (c >= r)).astype(F32)
    eye = (r == c).astype(F32)
    strict_f = inc_f - eye
    ones = jnp.ones((cl, HEAD_DIM), F32)
    n_chunks = SCAN_ROWS // cl
    order = range(n_chunks - 1, -1, -1) if reverse else range(n_chunks)
    heads = range(GDN_HEADS)
    items = [(h, ci) for ci in order for h in heads]

    def load(ref, lead):
        return [ref[lead + (slice(ci * cl, (ci + 1) * cl), slice(h * HEAD_DIM, (h + 1) * HEAD_DIM))]
                for h, ci in items]

    q, k, v = load(q_ref, (0, 0)), load(k_ref, (0, 0)), load(v_ref, (0, 0))
    gc, bc = load(g_ref, (0,)), load(b_ref, (0,))
    n = range(len(items))
    gcum = [_dot(inc_f, gc[i], precision=hi) for i in n]
    g_rows = [_dot(ones, gc[i], _NT, precision=hi) * (1.0 / HEAD_DIM) for i in n]
    gcum_row = [_dot(g_rows[i], inc_t, precision=hi) for i in n]
    gtot = [jnp.sum(gc[i], axis=0, keepdims=True) for i in n]
    decay = [jnp.where(inc, jnp.exp(jnp.where(inc, gcum[i][:, :cl] - gcum_row[i], 0.0)), 0.0) for i in n]
    eg = [jnp.exp(gcum[i]) for i in n]
    kb = [k[i] * bc[i] for i in n]
    a_pow = [_dot(kb[i], k[i], _NT) * decay[i] * strict_f for i in n]
    t_mat = [eye - a_pow[i] for i in n]
    for _ in range(int(math.log2(cl)) - 1):
        a_pow = [_dot(a_pow[i], a_pow[i]) for i in n]
        t_mat = [t_mat[i] + _dot(t_mat[i], a_pow[i]) for i in n]
    u = [_dot(t_mat[i], v[i] * bc[i]) for i in n]
    w = [_dot(t_mat[i], kb[i] * eg[i]) for i in n]
    attn = [_dot(q[i], k[i], _NT) * decay[i] for i in n]
    qd = [q[i] * eg[i] for i in n]
    kd = [k[i] * jnp.exp(gtot[i] - gcum[i]) for i in n]
    cd = [jnp.exp(gtot[i]) for i in n]
    st = [state_ref[h] for h in heads]
    for pos, ci in enumerate(order):
        idx = [pos * GDN_HEADS + h for h in heads]
        v_new = [u[i] - _dot(w[i], st[h]) for h, i in zip(heads, idx)]
        out = [_dot(qd[i], st[h]) + _dot(attn[i], v_new[h]) for h, i in zip(heads, idx)]
        st = [st[h] * cd[i] + _dot(kd[i], v_new[h], _TN) for h, i in zip(heads, idx)]
        for h in heads:
            o_ref[0, ci * cl:(ci + 1) * cl, h * HEAD_DIM:(h + 1) * HEAD_DIM] = out[h]
    for h in heads:
        state_ref[h] = st[h]


def gdn_scan(qkv, g, beta, n_lat, reverse):
    _, b, n, _ = qkv.shape
    nb, nlb = n // SCAN_ROWS, n_lat // SCAN_ROWS

    def blk(s):
        return _scan_block(s, nb, nlb, reverse)

    def sec(j):
        return pl.BlockSpec((1, 1, SCAN_ROWS, GDN_W), lambda bi, s: (j, bi, blk(s), 0))

    row = pl.BlockSpec((1, SCAN_ROWS, GDN_W), lambda bi, s: (bi, blk(s), 0))
    return pl.pallas_call(
        functools.partial(_gdn_scan_kernel, reverse=reverse),
        grid=(b, nb),
        in_specs=[sec(0), sec(1), sec(2), row, row],
        out_specs=row,
        out_shape=jax.ShapeDtypeStruct((b, n, GDN_W), F32),
        scratch_shapes=[pltpu.VMEM((GDN_HEADS, HEAD_DIM, HEAD_DIM), F32)],
        compiler_params=pltpu.CompilerParams(
            dimension_semantics=("parallel", "arbitrary"), vmem_limit_bytes=VMEM_LIMIT),
    )(qkv, qkv, qkv, g, beta)


def gdn_mixer(p_gdn, p_sml, lp, n_lat):
    b, n, _ = p_gdn.shape
    p_gdn2 = p_gdn.reshape(b * n, 4 * GDN_W)
    qkv = gdn_conv(p_gdn2, lp['conv_w'], n, n_lat).reshape(3, b, n, GDN_W)
    g_f, g_b, be_f, be_b = (a.reshape(b, n, GDN_W) for a in
                            gdn_gates(p_sml.reshape(b * n, -1), lp['gdn_a_log'], lp['gdn_dt_bias']))
    o_f = gdn_scan(qkv, g_f, be_f, n_lat, False)
    o_b = gdn_scan(qkv, g_b, be_b, n_lat, True)
    out = mixer_finalize(_gdn_final_kernel, o_f.reshape(b * n, GDN_W), o_b.reshape(b * n, GDN_W),
                         p_gdn2, 3, lp['gdn_norm_w'][None, :])
    return out.reshape(b, n, GDN_W)


def rms_norm(x, w):
    xf = x.astype(F32)
    y = xf * lax.rsqrt(jnp.mean(xf * xf, axis=-1, keepdims=True) + NORM_EPS)
    return (y * w.astype(F32)).astype(x.dtype)


def split_cols(t, sizes):
    parts, start = [], 0
    for s in sizes:
        parts.append(t[..., start:start + s])
        start += s
    return parts


def axial_rope_tables(row, col, rot_dim):
    n_freq = rot_dim // 4
    inv_freq = ROPE_BASE ** (-jnp.arange(n_freq, dtype=F32) / n_freq)
    ang = jnp.concatenate([row[:, None] * inv_freq, col[:, None] * inv_freq], axis=-1)
    return jnp.cos(ang), jnp.sin(ang)


MLA_HP = 2 * LANES
MLA_TT = 256
MLA_TQ = 512
MLA_TK = 4096
MLA_Q_SCALE = MLA_QK ** -0.5 * math.log2(math.e)


def _rope_tail(tail, c, s1, s2):
    half = ROPE_DIM // 2
    return tail * c + pltpu.roll(tail, LANES - half, 1) * s1 + pltpu.roll(tail, half, 1) * s2


def _mla_q_prep_kernel(x_ref, w_ref, c_ref, s1_ref, s2_ref, o_ref):
    c, s1, s2 = c_ref[...], s1_ref[...], s2_ref[...]
    for h in range(MLA_HEADS):
        lo = h * MLA_HP
        x = x_ref[:, lo:lo + MLA_HP]
        ms = jnp.sum(x * x, axis=1, keepdims=True) * (1.0 / MLA_QK)
        y = x * lax.rsqrt(ms + NORM_EPS) * w_ref[...]
        tail = _rope_tail(y[:, NOPE_DIM:], c, s1, s2)
        o_ref[:, lo:lo + NOPE_DIM] = (y[:, :NOPE_DIM] * MLA_Q_SCALE).astype(o_ref.dtype)
        o_ref[:, lo + NOPE_DIM:lo + MLA_HP] = (tail * MLA_Q_SCALE).astype(o_ref.dtype)


def _mla_kv_prep_kernel(kv_ref, kr_ref, w_ref, c_ref, s1_ref, s2_ref, k_ref, v_ref):
    c, s1, s2 = c_ref[...], s1_ref[...], s2_ref[...]
    kr = kr_ref[...]
    kr_ss = jnp.sum(kr * kr, axis=1, keepdims=True)
    for h in range(MLA_HEADS):
        lo = h * MLA_HP
        nope = kv_ref[:, lo:lo + NOPE_DIM]
        ms = (jnp.sum(nope * nope, axis=1, keepdims=True) + kr_ss) * (1.0 / MLA_QK)
        r = lax.rsqrt(ms + NORM_EPS)
        k_ref[:, lo:lo + NOPE_DIM] = (nope * r * w_ref[:, :NOPE_DIM]).astype(k_ref.dtype)
        tail = _rope_tail(kr * r * w_ref[:, NOPE_DIM:], c, s1, s2)
        k_ref[:, lo + NOPE_DIM:lo + MLA_HP] = tail.astype(k_ref.dtype)
        v_ref[:, h * V_DIM:(h + 1) * V_DIM] = kv_ref[:, lo + NOPE_DIM:lo + NOPE_DIM + V_DIM].astype(v_ref.dtype)


def mla_prep(qraw, kvraw, kr_pad, qw_pad, kw_pad, tabs, n_tot):
    t = qraw.shape[0]
    tiles = n_tot // MLA_TT
    tab_spec = pl.BlockSpec((MLA_TT, LANES), lambda i: (i % tiles, 0))
    w_spec = pl.BlockSpec((1, MLA_HP), lambda i: (0, 0))
    wide_spec = pl.BlockSpec((MLA_TT, MLA_HEADS * MLA_HP), lambda i: (i, 0))
    params = pltpu.CompilerParams(dimension_semantics=("parallel",), vmem_limit_bytes=VMEM_LIMIT)
    q = pl.pallas_call(
        _mla_q_prep_kernel,
        grid=(t // MLA_TT,),
        in_specs=[wide_spec, w_spec, tab_spec, tab_spec, tab_spec],
        out_specs=wide_spec,
        out_shape=jax.ShapeDtypeStruct((t, MLA_HEADS * MLA_HP), BF16),
        compiler_params=params,
    )(qraw, qw_pad, *tabs)
    k, v = pl.pallas_call(
        _mla_kv_prep_kernel,
        grid=(t // MLA_TT,),
        in_specs=[wide_spec, pl.BlockSpec((MLA_TT, LANES), lambda i: (i, 0)), w_spec,
                  tab_spec, tab_spec, tab_spec],
        out_specs=[wide_spec, pl.BlockSpec((MLA_TT, MLA_HEADS * V_DIM), lambda i: (i, 0))],
        out_shape=[jax.ShapeDtypeStruct((t, MLA_HEADS * MLA_HP), BF16),
                   jax.ShapeDtypeStruct((t, MLA_HEADS * V_DIM), BF16)],
        compiler_params=params,
    )(kvraw, kr_pad, kw_pad, *tabs)
    return q, k, v


def _mla_flash_kernel(q_ref, k_ref, v_ref, o_ref, *, n_loop, tail_len):
    q = q_ref[0]
    tq = q.shape[0]

    def attend(kc, vc, carry):
        m, l, acc = carry
        s = lax.dot_general(q, kc, (((1,), (1,)), ((), ())), preferred_element_type=F32)
        m_new = jnp.maximum(m, jnp.max(s, axis=1, keepdims=True))
        alpha = jnp.exp2(m - m_new)
        p = jnp.exp2(s - m_new)
        l = alpha * l + jnp.sum(p, axis=1, keepdims=True)
        acc = alpha * acc + jnp.dot(p.astype(vc.dtype), vc, preferred_element_type=F32)
        return m_new, l, acc

    def body(c, carry):
        off = pl.multiple_of(c * MLA_TK, MLA_TK)
        return attend(k_ref[0, pl.ds(off, MLA_TK), :], v_ref[0, pl.ds(off, MLA_TK), :], carry)

    carry = (jnp.full((tq, 1), -jnp.inf, F32), jnp.zeros((tq, 1), F32), jnp.zeros((tq, V_DIM), F32))
    if n_loop:
        carry = lax.fori_loop(0, n_loop, body, carry)
    if tail_len:
        lo = n_loop * MLA_TK
        carry = attend(k_ref[0, lo:lo + tail_len, :], v_ref[0, lo:lo + tail_len, :], carry)
    _, l, acc = carry
    o_ref[0] = (acc / l).astype(o_ref.dtype)


def mla_attend(q, k, v, n_lat, n_ctx, context_queries):
    b = q.shape[0]
    params = pltpu.CompilerParams(dimension_semantics=("parallel", "parallel", "arbitrary"),
                                  vmem_limit_bytes=VMEM_LIMIT)
    if context_queries:
        blk = n_lat // n_ctx
        return pl.pallas_call(
            functools.partial(_mla_flash_kernel, n_loop=0, tail_len=n_ctx),
            grid=(b, MLA_HEADS, 1),
            in_specs=[pl.BlockSpec((1, n_ctx, MLA_HP), lambda bi, h, i: (bi, blk, h)),
                      pl.BlockSpec((1, n_ctx, MLA_HP), lambda bi, h, i: (bi, blk, h)),
                      pl.BlockSpec((1, n_ctx, V_DIM), lambda bi, h, i: (bi, blk, h))],
            out_specs=pl.BlockSpec((1, n_ctx, V_DIM), lambda bi, h, i: (bi, 0, h)),
            out_shape=jax.ShapeDtypeStruct((b, n_ctx, MLA_HEADS * V_DIM), BF16),
            compiler_params=params,
        )(q, k, v)
    n_tot = n_lat + n_ctx
    assert n_lat % MLA_TK == 0 and n_lat % MLA_TQ == 0
    return pl.pallas_call(
        functools.partial(_mla_flash_kernel, n_loop=n_lat // MLA_TK, tail_len=n_ctx),
        grid=(b, MLA_HEADS, n_lat // MLA_TQ),
        in_specs=[pl.BlockSpec((1, MLA_TQ, MLA_HP), lambda bi, h, i: (bi, i, h)),
                  pl.BlockSpec((1, n_tot, MLA_HP), lambda bi, h, i: (bi, 0, h)),
                  pl.BlockSpec((1, n_tot, V_DIM), lambda bi, h, i: (bi, 0, h))],
        out_specs=pl.BlockSpec((1, MLA_TQ, V_DIM), lambda bi, h, i: (bi, i, h)),
        out_shape=jax.ShapeDtypeStruct((b, n_lat, MLA_HEADS * V_DIM), BF16),
        compiler_params=params,
    )(q, k, v)


def mla_tables(cos, sin, n_ctx):
    n_lat, half = cos.shape
    z = jnp.zeros((n_lat, LANES - 2 * half), F32)
    zh = jnp.zeros((n_lat, half), F32)
    c = jnp.concatenate([cos, cos, z], axis=1)
    s1 = jnp.concatenate([-sin, zh, z], axis=1)
    s2 = jnp.concatenate([zh, sin, z], axis=1)
    ident = jnp.concatenate([jnp.ones((n_ctx, 2 * half), F32), jnp.zeros((n_ctx, LANES - 2 * half), F32)], axis=1)
    zero = jnp.zeros((n_ctx, LANES), F32)
    return (jnp.concatenate([c, ident], axis=0), jnp.concatenate([s1, zero], axis=0),
            jnp.concatenate([s2, zero], axis=0))


def mla_mixer(c_q, c_kv, k_rope, lp, tabs, n_lat, with_ctx):
    b, n, _ = c_q.shape
    n_ctx = n - n_lat
    qraw = matmul(rms_norm(c_q, lp['cq_norm_w']).reshape(b * n, Q_LORA).astype(BF16), lp['w_uq_pad'])
    kvraw = matmul(rms_norm(c_kv, lp['ckv_norm_w']).reshape(b * n, KV_LORA).astype(BF16), lp['w_ukv'])
    kr_pad = jnp.pad(k_rope.reshape(b * n, ROPE_DIM), ((0, 0), (0, LANES - ROPE_DIM)))
    q, k, v = mla_prep(qraw, kvraw, kr_pad, lp['q_norm_pad'], lp['k_norm_pad'], tabs, n)
    q = q.reshape(b, n, -1)
    k = k.reshape(b, n, -1)
    v = v.reshape(b, n, -1)
    o_x = mla_attend(q, k, v, n_lat, n_ctx, False)
    if not with_ctx:
        return o_x
    return jnp.concatenate([o_x, mla_attend(q, k, v, n_lat, n_ctx, True)], axis=1)


RN_TT = 256


def _resid_norm_kernel(*refs, has_resid, has_norm):
    refs = list(refs)
    x_ref = refs.pop(0)
    y_ref = refs.pop(0) if has_resid else None
    vec_ref = refs.pop(0)
    w_ref = refs.pop(0) if has_norm else None
    xo_ref = refs.pop(0) if has_resid else None
    h_ref = refs.pop(0) if has_norm else None
    x = x_ref[0]
    if has_resid:
        x = x + vec_ref[0, 0, 0:1, :] * y_ref[0].astype(F32)
        xo_ref[0] = x
    if has_norm:
        ms = jnp.mean(x * x, axis=1, keepdims=True)
        hn = x * lax.rsqrt(ms + NORM_EPS) * w_ref[...]
        h_ref[0] = (hn * (1.0 + vec_ref[0, 0, 2:3, :]) + vec_ref[0, 0, 1:2, :]).astype(h_ref.dtype)


def resid_norm(x, y, vec, norm_w, n_rows, n_lat):
    b, _, d = x.shape
    lat_tiles = n_lat // RN_TT
    row = pl.BlockSpec((1, RN_TT, d), lambda bi, i: (bi, i, 0))
    in_specs, args, out_specs, out_shape = [row], [x], [], []
    if y is not None:
        in_specs.append(row)
        args.append(y)
        out_specs.append(row)
        out_shape.append(jax.ShapeDtypeStruct((b, n_rows, d), F32))
    in_specs.append(pl.BlockSpec((1, 1, SUBLANES, d), lambda bi, i: (bi, (i >= lat_tiles).astype(jnp.int32), 0, 0)))
    args.append(vec)
    if norm_w is not None:
        in_specs.append(pl.BlockSpec((1, d), lambda bi, i: (0, 0)))
        args.append(norm_w[None, :])
        out_specs.append(row)
        out_shape.append(jax.ShapeDtypeStruct((b, n_rows, d), BF16))
    return pl.pallas_call(
        functools.partial(_resid_norm_kernel, has_resid=y is not None, has_norm=norm_w is not None),
        grid=(b, n_rows // RN_TT),
        in_specs=in_specs, out_specs=out_specs, out_shape=out_shape,
        compiler_params=pltpu.CompilerParams(
            dimension_semantics=("parallel", "parallel"), vmem_limit_bytes=VMEM_LIMIT),
    )(*args)


def mod_vec(gate_x, gate_c, shift_x, shift_c, scale_x, scale_c):
    lat = jnp.stack([gate_x, shift_x, scale_x], axis=1)
    con = jnp.broadcast_to(jnp.stack([gate_c, shift_c, scale_c], axis=0)[None], lat.shape)
    v = jnp.stack([lat, con], axis=1)
    return jnp.pad(v, ((0, 0), (0, 0), (0, SUBLANES - v.shape[2]), (0, 0)))


def trunk_layer(xs, pending, mod_x, mod_c, lp, ret_tabs, mla_tabs, n_lat, last):
    b, n_tot, d = xs.shape
    sh1, sc1, g1, sh2, sc2, g2 = jnp.split(mod_x, 6, axis=-1)
    csh1, csc1, cg1, csh2, csc2, cg2 = jnp.split(mod_c, 6, axis=-1)
    if pending is None:
        (h,) = resid_norm(xs, None, mod_vec(g1, cg1, sh1, csh1, sc1, csc1), lp['norm1_w'], n_tot, n_lat)
    else:
        y_prev, pg, pcg = pending
        xs, h = resid_norm(xs, y_prev, mod_vec(pg, pcg, sh1, csh1, sc1, csc1), lp['norm1_w'], n_tot, n_lat)
    h2 = h.reshape(b * n_tot, d)
    p_ret = matmul(h2, lp['w_in_ret']).reshape(b, n_tot, 4 * RET_W)
    p_gdn = matmul(h2, lp['w_in_gdn']).reshape(b, n_tot, 4 * GDN_W)
    p_sml = matmul(h2, lp['w_in_sml'], tn=2048).reshape(b, n_tot, -1)
    _, _, _, _, c_q, c_kv, k_rope = split_cols(p_sml, (GDN_HEADS,) * 4 + (Q_LORA, KV_LORA, ROPE_DIM))
    ret_o = retention_mixer(p_ret, lp['ret_norm_w'], ret_tabs, n_lat)
    gdn_o = gdn_mixer(p_gdn, p_sml, lp, n_lat)
    mla_o = mla_mixer(c_q, c_kv, k_rope, lp, mla_tabs, n_lat, not last)
    n_out = n_lat if last else n_tot
    mixed = jnp.concatenate([ret_o[:, :n_out], gdn_o[:, :n_out], mla_o], axis=-1)
    y = matmul(mixed.reshape(b * n_out, MIX_W), lp['w_out'], BF16).reshape(b, n_out, d)
    xs, f = resid_norm(xs, y, mod_vec(g1, cg1, sh2, csh2, sc2, csc2), lp['norm2_w'], n_out, n_lat)
    f = peer_ffn(f.reshape(b * n_out, d), lp['peer_wq'], lp['peer_keys'], lp['peer_u'], lp['peer_vt'])
    return xs, (f.reshape(b, n_out, d), g2, cg2)


def kernel(x, c, ctx, c_ctx, ada_w, ada_b, norm1_w, norm2_w, w_in, conv_w, gdn_a_log, gdn_dt_bias,
           gdn_norm_w, ret_norm_w, cq_norm_w, ckv_norm_w, w_uq, w_ukv, q_norm_w, k_norm_w, w_out,
           peer_wq, peer_keys, peer_u, peer_v):
    n_lat = x.shape[1]
    n_rows = n_lat // GRID_W
    row = jnp.repeat(jnp.arange(n_rows, dtype=F32), GRID_W, total_repeat_length=n_lat)
    col = (jnp.arange(n_lat) % GRID_W).astype(F32)
    ret_tabs = ret_tables(*axial_rope_tables(row, col, HEAD_DIM), ctx.shape[1])
    mla_tabs = mla_tables(*axial_rope_tables(row, col, ROPE_DIM), ctx.shape[1])
    silu_c = jax.nn.silu(c)
    silu_cc = jax.nn.silu(c_ctx)
    sml_w = 4 * GDN_HEADS + Q_LORA + KV_LORA + ROPE_DIM
    sml_pad = -sml_w % LANES
    xs = jnp.concatenate([x, ctx], axis=1)
    pending = None
    for i in range(DEPTH):
        mod_x = silu_c @ ada_w[i] + ada_b[i]
        mod_c = silu_cc @ ada_w[i] + ada_b[i]
        w_in_i = w_in[i].astype(BF16)
        lp = {
            'norm1_w': norm1_w[i], 'norm2_w': norm2_w[i], 'conv_w': conv_w[i],
            'w_in_ret': w_in_i[:, :4 * RET_W],
            'w_in_gdn': w_in_i[:, 4 * RET_W:4 * RET_W + 4 * GDN_W],
            'w_in_sml': jnp.pad(w_in_i[:, 4 * RET_W + 4 * GDN_W:], ((0, 0), (0, sml_pad))),
            'gdn_a_log': gdn_a_log[i], 'gdn_dt_bias': gdn_dt_bias[i], 'gdn_norm_w': gdn_norm_w[i],
            'ret_norm_w': ret_norm_w[i], 'cq_norm_w': cq_norm_w[i], 'ckv_norm_w': ckv_norm_w[i],
            'w_uq_pad': jnp.pad(w_uq[i].astype(BF16).reshape(Q_LORA, MLA_HEADS, MLA_QK),
                                ((0, 0), (0, 0), (0, MLA_HP - MLA_QK))).reshape(Q_LORA, MLA_HEADS * MLA_HP),
            'w_ukv': w_ukv[i].astype(BF16),
            'q_norm_pad': jnp.pad(q_norm_w[i], (0, MLA_HP - MLA_QK))[None, :],
            'k_norm_pad': jnp.pad(k_norm_w[i], (0, MLA_HP - MLA_QK))[None, :],
            'w_out': w_out[i].astype(BF16), 'peer_wq': peer_wq[i].astype(BF16), 'peer_keys': peer_keys[i],
            'peer_u': peer_u[i].astype(BF16), 'peer_vt': peer_v[i].astype(BF16).T,
        }
        xs, pending = trunk_layer(xs, pending, mod_x, mod_c, lp, ret_tabs, mla_tabs, n_lat, i == DEPTH - 1)
    f, g2, cg2 = pending
    (out,) = resid_norm(xs, f, mod_vec(g2, cg2, g2, cg2, g2, cg2), None, n_lat, n_lat)
    return out
```
